```python
import jax
import jax.numpy as jnp
from jax import lax
import numpy as np

D_MODEL = 2048
BATCH = 16
SEQ = 2048
DEPTH = 1
DEC_BATCH = 32
DEC_SEQ = 4
PAST_LEN = 16384
PAGE_SIZE = 128

N_META = 16
MIX_WIDTH = D_MODEL
HEAD_DIM = 128
ATTN_HEADS = (MIX_WIDTH // 2) // HEAD_DIM
ATTN_WIDTH = ATTN_HEADS * HEAD_DIM
IDX_HEADS = 8
IDX_DIM = 64
TOPK_MAX = 256
ATTN_QBLOCK = 16
ROPE_THETA = 500000.0
DELTA_DK = 128
DELTA_DV = 128
DELTA_WIDTH = MIX_WIDTH - ATTN_WIDTH
DELTA_HEADS = DELTA_WIDTH // DELTA_DV
CONV_WIDTH = 4
DELTA_CHUNK = 64
PEER_HEADS = 8
PEER_NKEYS = 128
PEER_QDIM = 256
PEER_TOPK_HALF = 16
PEER_TOPK = 16
N_EXPERTS = PEER_NKEYS * PEER_NKEYS
PEER_BLOCK = 128
EPS = 1e-6
NEG = -1e30
SPLIT_SIZES = (ATTN_WIDTH, ATTN_WIDTH, ATTN_WIDTH, IDX_HEADS * IDX_DIM, IDX_DIM, IDX_HEADS,
               3 * DELTA_WIDTH, DELTA_WIDTH, DELTA_HEADS, DELTA_HEADS)
PROJ_WIDTH = sum(SPLIT_SIZES)

kernel_name = 'hymba_dsa_gdn_peer_step'

F32 = jnp.float32


def rmsnorm(x, gain):
    xf = x.astype(F32)
    y = xf * lax.rsqrt(jnp.mean(xf * xf, axis=-1, keepdims=True) + EPS) * gain.astype(F32)
    return y.astype(x.dtype)


def l2norm(x):
    xf = x.astype(F32)
    return xf * lax.rsqrt(jnp.sum(xf * xf, axis=-1, keepdims=True) + EPS)


def rope(x, pos):
    rot = x.shape[-1] // 4
    half = rot // 2
    inv = ROPE_THETA ** (-jnp.arange(half, dtype=F32) * 2.0 / rot)
    ang = pos.astype(F32)[:, None] * inv[None, :]
    cos = jnp.cos(ang)[None, :, None, :]
    sin = jnp.sin(ang)[None, :, None, :]
    x1 = x[..., :half].astype(F32)
    x2 = x[..., half:rot].astype(F32)
    r = jnp.concatenate([x1 * cos - x2 * sin, x2 * cos + x1 * sin], axis=-1).astype(x.dtype)
    return jnp.concatenate([r, x[..., rot:]], axis=-1)


def project(h, pos, conv_prev, w_in, conv_w, a_log, dt_bias):
    B, T, _ = h.shape
    p = h @ w_in
    parts = []
    off = 0
    for size in SPLIT_SIZES:
        parts.append(p[..., off:off + size])
        off += size
    q, k, v, qi, ki, wi, dqkv, z, b, a = parts
    q = rope(q.reshape(B, T, ATTN_HEADS, HEAD_DIM), pos)
    k = rope(k.reshape(B, T, ATTN_HEADS, HEAD_DIM), pos)
    v = v.reshape(B, T, ATTN_HEADS, HEAD_DIM)
    qi = rope(qi.reshape(B, T, IDX_HEADS, IDX_DIM), pos)
    ki = rope(ki[:, :, None, :], pos)[:, :, 0, :]
    wi = wi.astype(F32) * (IDX_HEADS * IDX_DIM) ** -0.5
    xpad = jnp.concatenate([conv_prev.astype(dqkv.dtype), dqkv], axis=1)
    conv = sum(conv_w[i] * xpad[:, i:i + T] for i in range(CONV_WIDTH))
    conv = jax.nn.silu(conv)
    new_conv = xpad[:, T:]
    dq = conv[..., :DELTA_WIDTH].reshape(B, T, DELTA_HEADS, DELTA_DK)
    dk = conv[..., DELTA_WIDTH:2 * DELTA_WIDTH].reshape(B, T, DELTA_HEADS, DELTA_DK)
    dv = conv[..., 2 * DELTA_WIDTH:].reshape(B, T, DELTA_HEADS, DELTA_DV).astype(F32)
    dq = l2norm(dq) * DELTA_DK ** -0.5
    dk = l2norm(dk)
    beta = jax.nn.sigmoid(b.astype(F32))
    g = -jnp.exp(a_log.astype(F32)) * jax.nn.softplus(a.astype(F32) + dt_bias.astype(F32))
    return (q, k, v, qi, ki, wi), (dq, dk, dv, g, beta, z), new_conv


def index_scores(qi, ki, wi):
    s = jax.nn.relu(jnp.einsum('bqhd,bsd->bqhs', qi, ki).astype(F32))
    return jnp.einsum('bqhs,bqh->bqs', s, wi)


def attend_selected(q, ks, vs, valid):
    logits = jnp.einsum('bqhd,bqkhd->bqhk', q, ks).astype(F32) * HEAD_DIM ** -0.5
    logits = jnp.where(valid[:, :, None, :], logits, NEG)
    p = jax.nn.softmax(logits, axis=-1)
    return jnp.einsum('bqhk,bqkhd->bqhd', p.astype(vs.dtype), vs)


def sparse_attn_prompt(q, k, v, qi, ki, wi, topk):
    B, T, H, hd = q.shape
    nb = T // ATTN_QBLOCK
    pos = jnp.arange(T)
    bidx = jnp.arange(B)[:, None, None]

    def blocks(a):
        return jnp.moveaxis(a.reshape((B, nb, ATTN_QBLOCK) + a.shape[2:]), 1, 0)

    def one_block(args):
        qb, qib, wib, pb = args
        scores = index_scores(qib, ki, wib)
        scores = jnp.where((pos[None, :] <= pb[:, None])[None], scores, NEG)
        _, idx = lax.top_k(scores, topk)
        valid = idx <= pb[None, :, None]
        return attend_selected(qb, k[bidx, idx], v[bidx, idx], valid)

    out = lax.map(one_block, (blocks(q), blocks(qi), blocks(wi), pos.reshape(nb, ATTN_QBLOCK)))
    return jnp.moveaxis(out, 0, 1).reshape(B, T, H * hd)


def sparse_attn_sample(q, k, v, qi, ki, wi, cache_k, cache_v, cache_kidx, page_table, pos, topk):
    DB, DS, H, hd = q.shape
    past = page_table.shape[1] * PAGE_SIZE
    bidx = jnp.arange(DB)[:, None, None]
    ki_past = cache_kidx[page_table].reshape(DB, past, IDX_DIM)
    ki_all = jnp.concatenate([ki_past, ki.astype(ki_past.dtype)], axis=1)
    key_pos = jnp.arange(past + DS)
    scores = index_scores(qi, ki_all, wi)
    scores = jnp.where((key_pos[None, :] <= pos[:, None])[None], scores, NEG)
    _, idx = lax.top_k(scores, topk)
    valid = idx <= pos[None, :, None]
    in_past = (idx < past)[..., None, None]
    pidx = jnp.minimum(idx, past - 1)
    page = page_table[bidx, pidx // PAGE_SIZE]
    off = pidx % PAGE_SIZE
    nidx = jnp.clip(idx - past, 0, DS - 1)
    ks = jnp.where(in_past, cache_k[page, off], k[bidx, nidx].astype(cache_k.dtype))
    vs = jnp.where(in_past, cache_v[page, off], v[bidx, nidx].astype(cache_v.dtype))
    return attend_selected(q, ks, vs, valid).reshape(DB, DS, H * hd)


def gated_delta(q, k, v, g, beta, s0, chunk):
    B, T, H, DK = q.shape
    DV = v.shape[-1]
    n = -(-T // chunk)
    pad = n * chunk - T

    def to_chunks(a):
        a = jnp.pad(a.astype(F32), [(0, 0), (0, pad)] + [(0, 0)] * (a.ndim - 2))
        return jnp.moveaxis(a.reshape((B, n, chunk) + a.shape[2:]), 1, 0)

    incl = jnp.tril(jnp.ones((chunk, chunk), dtype=bool))
    strict = jnp.tril(jnp.ones((chunk, chunk), dtype=bool), -1)
    eye = jnp.eye(chunk, dtype=F32)

    def step(S, inp):
        qc, kc, vc, gc, bc = inp
        G = jnp.cumsum(gc, axis=1).transpose(0, 2, 1)
        bh = bc.transpose(0, 2, 1)
        diff = G[..., :, None] - G[..., None, :]
        dec = jnp.where(incl, jnp.exp(jnp.where(incl, diff, 0.0)), 0.0)
        kk = jnp.einsum('bihd,bjhd->bhij', kc, kc)
        A = jnp.where(strict, kk * dec, 0.0) * bh[..., :, None]
        eG = jnp.exp(G)
        rhs = bh[..., None] * (vc.transpose(0, 2, 1, 3) - eG[..., None] * jnp.einsum('bihd,bhde->bhie', kc, S))
        U = lax.linalg.triangular_solve(eye + A, rhs, left_side=True, lower=True, unit_diagonal=True)
        qk = jnp.einsum('bihd,bjhd->bhij', qc, kc) * dec
        o = eG[..., None] * jnp.einsum('bihd,bhde->bhie', qc, S) + jnp.einsum('bhij,bhje->bhie', qk, U)
        Gl = G[..., -1]
        S_new = jnp.exp(Gl)[..., None, None] * S + jnp.einsum('bjhd,bhj,bhje->bhde', kc, jnp.exp(Gl[..., None] - G), U)
        return S_new, o.transpose(0, 2, 1, 3)

    S, o = lax.scan(step, s0.astype(F32), (to_chunks(q), to_chunks(k), to_chunks(v), to_chunks(g), to_chunks(beta)))
    o = jnp.moveaxis(o, 0, 1).reshape(B, n * chunk, H, DV)[:, :T]
    return o, S


def combine(attn_out, o_delta, z, delta_norm, w_out):
    B, T, _ = attn_out.shape
    of = o_delta.astype(F32)
    on = of * lax.rsqrt(jnp.mean(of * of, axis=-1, keepdims=True) + EPS) * delta_norm.astype(F32)
    gated = (on * jax.nn.silu(z.astype(F32).reshape(B, T, DELTA_HEADS, DELTA_DV))).reshape(B, T, DELTA_WIDTH)
    cat = jnp.concatenate([attn_out, gated.astype(attn_out.dtype)], axis=-1)
    return cat @ w_out


def peer(x, w_q, subkeys, u_tab, v_tab):
    ntok, D = x.shape
    nb = -(-ntok // PEER_BLOCK)
    xb = jnp.pad(x, ((0, nb * PEER_BLOCK - ntok), (0, 0))).reshape(nb, PEER_BLOCK, D)

    def one_block(xt):
        q = (xt @ w_q).reshape(PEER_BLOCK, PEER_HEADS, 2, PEER_QDIM // 2)
        s = jnp.einsum('phcd,ckd->phck', q, subkeys).astype(F32)
        sv, si = lax.top_k(s, PEER_TOPK_HALF)
        cand = (sv[:, :, 0, :, None] + sv[:, :, 1, None, :]).reshape(PEER_BLOCK, PEER_HEADS, -1)
        cidx = (si[:, :, 0, :, None] * PEER_NKEYS + si[:, :, 1, None, :]).reshape(PEER_BLOCK, PEER_HEADS, -1)
        best, sel = lax.top_k(cand, PEER_TOPK)
        eidx = jnp.take_along_axis(cidx, sel, axis=-1)
        gate = jax.nn.softmax(best, axis=-1)
        h = jax.nn.gelu(jnp.einsum('pd,phkd->phk', xt, u_tab[eidx]).astype(F32))
        return jnp.einsum('phk,phkd->pd', (gate * h).astype(v_tab.dtype), v_tab[eidx])

    return lax.map(one_block, xb).reshape(nb * PEER_BLOCK, D)[:ntok]


def setup_inputs(seed: int = 0) -> dict:
    key = jax.random.key(seed)
    ks = jax.random.split(key, 24)
    n_pages = PAST_LEN // PAGE_SIZE
    n_pool = (DEC_BATCH * n_pages * 5) // 4
    nrm = jax.random.normal
    page_table = jax.random.permutation(ks[0], n_pool)[:DEC_BATCH * n_pages].reshape(DEC_BATCH, n_pages).astype(jnp.int32)
    return {
        'x_prompt': nrm(ks[1], (BATCH, SEQ, D_MODEL), F32),
        'x_sample': nrm(ks[2], (DEC_BATCH, DEC_SEQ, D_MODEL), F32),
        'cache_k': nrm(ks[3], (DEPTH, n_pool, PAGE_SIZE, ATTN_HEADS, HEAD_DIM), F32),
        'cache_v': nrm(ks[4], (DEPTH, n_pool, PAGE_SIZE, ATTN_HEADS, HEAD_DIM), F32),
        'cache_kidx': nrm(ks[5], (DEPTH, n_pool, PAGE_SIZE, IDX_DIM), F32),
        'state_conv': nrm(ks[6], (DEPTH, DEC_BATCH, CONV_WIDTH - 1, 3 * DELTA_WIDTH), F32) * 0.5,
        'state_ssm': nrm(ks[7], (DEPTH, DEC_BATCH, DELTA_HEADS, DELTA_DK, DELTA_DV), F32) * 0.1,
        'page_table': page_table,
        'meta': nrm(ks[8], (N_META, D_MODEL), F32),
        'norm_mix': 1.0 + 0.01 * nrm(ks[9], (DEPTH, D_MODEL), F32),
        'w_in': nrm(ks[10], (DEPTH, D_MODEL, PROJ_WIDTH), F32) * D_MODEL ** -0.5,
        'conv_w': nrm(ks[11], (DEPTH, CONV_WIDTH, 3 * DELTA_WIDTH), F32) * CONV_WIDTH ** -0.5,
        'a_log': jnp.log(jax.random.uniform(ks[12], (DEPTH, DELTA_HEADS), F32, 1.0, 16.0)),
        'dt_bias': 0.1 * nrm(ks[13], (DEPTH, DELTA_HEADS), F32),
        'delta_norm': 1.0 + 0.01 * nrm(ks[14], (DEPTH, DELTA_DV), F32),
        'w_out': nrm(ks[15], (DEPTH, MIX_WIDTH, D_MODEL), F32) * MIX_WIDTH ** -0.5,
        'norm_ffn': 1.0 + 0.01 * nrm(ks[16], (DEPTH, D_MODEL), F32),
        'peer_wq': nrm(ks[17], (DEPTH, D_MODEL, PEER_HEADS * PEER_QDIM), F32) * D_MODEL ** -0.5,
        'peer_subkeys': nrm(ks[18], (DEPTH, 2, PEER_NKEYS, PEER_QDIM // 2), F32) * (PEER_QDIM // 2) ** -0.5,
        'peer_u': nrm(ks[19], (DEPTH, N_EXPERTS, D_MODEL), F32) * D_MODEL ** -0.5,
        'peer_v': nrm(ks[20], (DEPTH, N_EXPERTS, D_MODEL), F32) * PEER_HEADS ** -0.5,
        'norm_final': 1.0 + 0.01 * nrm(ks[21], (D_MODEL,), F32),
    }


def reference(x_prompt, x_sample, cache_k, cache_v, cache_kidx, state_conv, state_ssm, page_table,
              meta, norm_mix, w_in, conv_w, a_log, dt_bias, delta_norm, w_out, norm_ffn,
              peer_wq, peer_subkeys, peer_u, peer_v, norm_final):
    B, S, D = x_prompt.shape
    DB, DS, _ = x_sample.shape
    T = S + N_META
    past = page_table.shape[1] * PAGE_SIZE
    topk_prompt = min(TOPK_MAX, S // 4)
    topk_sample = min(TOPK_MAX, (past + DS) // 4)
    xp = jnp.concatenate([jnp.broadcast_to(meta[None].astype(x_prompt.dtype), (B, N_META, D)), x_prompt], axis=1)
    xs = x_sample
    pos_p = jnp.arange(T)
    pos_s = past + jnp.arange(DS)
    kp_l, vp_l, kip_l, cp_l, sp_l = [], [], [], [], []
    ks_l, vs_l, kis_l, cs_l, ss_l = [], [], [], [], []
    for l in range(DEPTH):
        hp = rmsnorm(xp, norm_mix[l])
        (q, k, v, qi, ki, wi), (dq, dk, dv, g, beta, z), conv_p = project(
            hp, pos_p, jnp.zeros((B, CONV_WIDTH - 1, 3 * DELTA_WIDTH), hp.dtype), w_in[l], conv_w[l], a_log[l], dt_bias[l])
        attn_p = sparse_attn_prompt(q, k, v, qi, ki, wi, topk_prompt)
        s0 = jnp.zeros((B, DELTA_HEADS, DELTA_DK, DELTA_DV), F32)
        o_meta, s_meta = gated_delta(dq[:, :N_META], dk[:, :N_META], dv[:, :N_META], g[:, :N_META], beta[:, :N_META], s0, N_META)
        o_real, ssm_p = gated_delta(dq[:, N_META:], dk[:, N_META:], dv[:, N_META:], g[:, N_META:], beta[:, N_META:], s_meta, DELTA_CHUNK)
        o_p = jnp.concatenate([o_meta, o_real], axis=1)
        xp = xp + combine(attn_p, o_p, z, delta_norm[l], w_out[l])
        xp = xp + peer(rmsnorm(xp, norm_ffn[l]).reshape(B * T, D), peer_wq[l], peer_subkeys[l], peer_u[l], peer_v[l]).reshape(B, T, D)
        kp_l.append(k)
        vp_l.append(v)
        kip_l.append(ki)
        cp_l.append(conv_p)
        sp_l.append(ssm_p)
        hs = rmsnorm(xs, norm_mix[l])
        (q, k, v, qi, ki, wi), (dq, dk, dv, g, beta, z), conv_s = project(
            hs, pos_s, state_conv[l], w_in[l], conv_w[l], a_log[l], dt_bias[l])
        attn_s = sparse_attn_sample(q, k, v, qi, ki, wi, cache_k[l], cache_v[l], cache_kidx[l], page_table, pos_s, topk_sample)
        o_s, ssm_s = gated_delta(dq, dk, dv, g, beta, state_ssm[l], min(DELTA_CHUNK, DS))
        xs = xs + combine(attn_s, o_s, z, delta_norm[l], w_out[l])
        xs = xs + peer(rmsnorm(xs, norm_ffn[l]).reshape(DB * DS, D), peer_wq[l], peer_subkeys[l], peer_u[l], peer_v[l]).reshape(DB, DS, D)
        ks_l.append(k)
        vs_l.append(v)
        kis_l.append(ki)
        cs_l.append(conv_s)
        ss_l.append(ssm_s)
    y_prompt = rmsnorm(xp[:, N_META:], norm_final)
    y_sample = rmsnorm(xs, norm_final)
    k_prompt = jnp.stack(kp_l)
    v_prompt = jnp.stack(vp_l)
    kidx_prompt = jnp.stack(kip_l)
    conv_prompt = jnp.stack(cp_l)
    ssm_prompt = jnp.stack(sp_l)
    k_sample = jnp.stack(ks_l)
    v_sample = jnp.stack(vs_l)
    kidx_sample = jnp.stack(kis_l)
    conv_sample = jnp.stack(cs_l)
    ssm_sample = jnp.stack(ss_l)
    return (y_prompt, y_sample, k_prompt, v_prompt, kidx_prompt, conv_prompt, ssm_prompt,
            k_sample, v_sample, kidx_sample, conv_sample, ssm_sample)
```

```python
import functools
import math

import jax
import jax.numpy as jnp
from jax import lax
from jax.experimental import pallas as pl
from jax.experimental.pallas import tpu as pltpu

F32 = jnp.float32
MXU_DTYPE = jnp.bfloat16

N_META = 16
HEAD_DIM = 128
ATTN_HEADS = 8
ATTN_WIDTH = ATTN_HEADS * HEAD_DIM
IDX_HEADS = 8
IDX_DIM = 64
TOPK_MAX = 256
ROPE_THETA = 500000.0
DELTA_DK = 128
DELTA_DV = 128
DELTA_HEADS = 8
DELTA_WIDTH = DELTA_HEADS * DELTA_DV
CONV_WIDTH = 4
PEER_HEADS = 8
PEER_NKEYS = 128
PEER_QDIM = 256
PEER_TOPK = 16
PAGE_SIZE = 128
EPS = 1e-6
NEG = -1e30

COL_Q = 0
COL_K = COL_Q + ATTN_WIDTH
COL_V = COL_K + ATTN_WIDTH
COL_QI = COL_V + ATTN_WIDTH
COL_SMALL = COL_QI + IDX_HEADS * IDX_DIM
COL_DQKV = COL_SMALL + 128
COL_Z = COL_DQKV + 3 * DELTA_WIDTH
COL_END = COL_Z + DELTA_WIDTH
PROJ_PAD = 8192

TOKEN_BLOCK = 512
DELTA_CHUNK = 128
VMEM_LIMIT = 56 * 1024 * 1024

_NT = (((1,), (1,)), ((), ()))


def _round_up(x, m):
    return (x + m - 1) // m * m


def _cparams(*sem):
    return pltpu.CompilerParams(dimension_semantics=sem, vmem_limit_bytes=VMEM_LIMIT)


def _norm_matmul_kernel(x_ref, g_ref, w_ref, o_ref, xn_ref):
    @pl.when(pl.program_id(1) == 0)
    def _():
        x = x_ref[...]
        ms = jnp.mean(x * x, axis=-1, keepdims=True)
        xn_ref[...] = (x * lax.rsqrt(ms + EPS) * g_ref[...]).astype(xn_ref.dtype)

    o_ref[...] = jnp.dot(xn_ref[...], w_ref[...], preferred_element_type=F32).astype(o_ref.dtype)


def norm_matmul(x, gain, w, *, tm, tn, out_dtype, return_normed):
    n, d = x.shape
    width = w.shape[1]
    grid = (n // tm, width // tn)
    in_specs = [
        pl.BlockSpec((tm, d), lambda i, j: (i, 0)),
        pl.BlockSpec((1, d), lambda i, j: (0, 0)),
        pl.BlockSpec((d, tn), lambda i, j: (0, j)),
    ]
    o_spec = pl.BlockSpec((tm, tn), lambda i, j: (i, j))
    o_shape = jax.ShapeDtypeStruct((n, width), out_dtype)
    xn_spec = pl.BlockSpec((tm, d), lambda i, j: (i, 0))
    if return_normed:
        return pl.pallas_call(
            _norm_matmul_kernel, grid=grid, in_specs=in_specs,
            out_specs=[o_spec, xn_spec],
            out_shape=[o_shape, jax.ShapeDtypeStruct((n, d), MXU_DTYPE)],
            compiler_params=_cparams("parallel", "arbitrary"), name="norm_matmul_xn",
        )(x, gain.reshape(1, d).astype(F32), w)
    return pl.pallas_call(
        _norm_matmul_kernel, grid=grid, in_specs=in_specs, out_specs=o_spec, out_shape=o_shape,
        scratch_shapes=[pltpu.VMEM((tm, d), MXU_DTYPE)],
        compiler_params=_cparams("parallel", "arbitrary"), name="norm_matmul",
    )(x, gain.reshape(1, d).astype(F32), w)


def _matmul2_kernel(a1_ref, a2_ref, w1_ref, w2_ref, r_ref, o_ref):
    acc = jnp.dot(a1_ref[...], w1_ref[...], preferred_element_type=F32)
    acc = acc + jnp.dot(a2_ref[...], w2_ref[...], preferred_element_type=F32)
    o_ref[...] = acc + r_ref[...]


def matmul2_residual(a1, a2, w1, w2, r, *, tm, tn):
    n, k1 = a1.shape
    k2 = a2.shape[1]
    d = w1.shape[1]
    return pl.pallas_call(
        _matmul2_kernel, grid=(n // tm, d // tn),
        in_specs=[
            pl.BlockSpec((tm, k1), lambda i, j: (i, 0)),
            pl.BlockSpec((tm, k2), lambda i, j: (i, 0)),
            pl.BlockSpec((k1, tn), lambda i, j: (0, j)),
            pl.BlockSpec((k2, tn), lambda i, j: (0, j)),
            pl.BlockSpec((tm, tn), lambda i, j: (i, j)),
        ],
        out_specs=pl.BlockSpec((tm, tn), lambda i, j: (i, j)),
        out_shape=jax.ShapeDtypeStruct((n, d), F32),
        compiler_params=_cparams("parallel", "arbitrary"), name="out_proj",
    )(a1, a2, w1, w2, r)


def _order_key(s):
    s = jnp.where(s == 0.0, 0.0, s)
    key = lax.bitcast_convert_type(s, jnp.int32)
    return jnp.where(key < 0, key ^ jnp.int32(0x7FFFFFFF), key)


def _count(mask):
    return jnp.sum(jnp.where(mask, 1.0, 0.0), axis=-1, keepdims=True)


def _topk_mask(key, col, topk, idx_bits):
    kf = float(topk)
    rows = key.shape[0]

    nonneg = _count(key >= 0) >= kf
    base = jnp.where(nonneg, jnp.int32(0), jnp.int32(-2 ** 31))

    def value_bit(it, base):
        cand = base | jnp.left_shift(jnp.int32(1), 30 - it)
        return jnp.where(_count(key >= cand) >= kf, cand, base)

    thr = lax.fori_loop(0, 31, value_bit, base)
    above = key > thr
    tied = key == thr
    need = kf - _count(above)

    def index_bit(it, j):
        cand = j | jnp.left_shift(jnp.int32(1), idx_bits - 1 - it)
        return jnp.where(_count(tied & (col < cand)) < need, cand, j)

    j = lax.fori_loop(0, idx_bits, index_bit, jnp.zeros((rows, 1), jnp.int32))
    return above | (tied & (col <= j))


def _attn_prompt_kernel(q_ref, k_ref, v_ref, qi_ref, ki_ref, wi_ref, o_ref, *, tq, topk, idx_bits):
    tp = k_ref.shape[1]
    ki = ki_ref[0]
    wi = wi_ref[0]
    s = jnp.zeros((tq, tp), F32)
    for h in range(IDX_HEADS):
        d = lax.dot_general(qi_ref[0, h], ki, _NT, preferred_element_type=F32)
        s = s + jnp.maximum(d, 0.0) * wi[:, h:h + 1]
    row = pl.program_id(1) * tq + lax.broadcasted_iota(jnp.int32, (tq, tp), 0)
    col = lax.broadcasted_iota(jnp.int32, (tq, tp), 1)
    causal = col <= row
    key = _order_key(jnp.where(causal, s, NEG))
    sel = _topk_mask(key, col, topk, idx_bits)
    bias = jnp.where(sel & causal, 0.0, NEG)
    scale = HEAD_DIM ** -0.5
    for h in range(ATTN_HEADS):
        hs = slice(h * HEAD_DIM, (h + 1) * HEAD_DIM)
        logits = lax.dot_general(q_ref[0, :, hs], k_ref[0, :, hs], _NT, preferred_element_type=F32) * scale + bias
        m = jnp.max(logits, axis=-1, keepdims=True)
        p = jnp.exp(logits - m)
        l = jnp.sum(p, axis=-1, keepdims=True)
        o = jnp.dot(p.astype(v_ref.dtype), v_ref[0, :, hs], preferred_element_type=F32)
        o_ref[0, :, hs] = (o / l).astype(o_ref.dtype)


def attn_prompt(q, k, v, qi, ki, wi, *, tq, topk):
    b, tp, w = q.shape
    idx_bits = max(1, (tp - 1).bit_length())
    kern = functools.partial(_attn_prompt_kernel, tq=tq, topk=topk, idx_bits=idx_bits)
    return pl.pallas_call(
        kern, grid=(b, tp // tq),
        in_specs=[
            pl.BlockSpec((1, tq, w), lambda bb, i: (bb, i, 0)),
            pl.BlockSpec((1, tp, w), lambda bb, i: (bb, 0, 0)),
            pl.BlockSpec((1, tp, w), lambda bb, i: (bb, 0, 0)),
            pl.BlockSpec((1, IDX_HEADS, tq, IDX_DIM), lambda bb, i: (bb, 0, i, 0)),
            pl.BlockSpec((1, tp, IDX_DIM), lambda bb, i: (bb, 0, 0)),
            pl.BlockSpec((1, tq, IDX_HEADS), lambda bb, i: (bb, i, 0)),
        ],
        out_specs=pl.BlockSpec((1, tq, w), lambda bb, i: (bb, i, 0)),
        out_shape=jax.ShapeDtypeStruct((b, tp, w), MXU_DTYPE),
        compiler_params=_cparams("parallel", "arbitrary"), name="attn_prompt",
    )(q, k, v, qi, ki, wi)


def _dotf(a, b):
    return jnp.dot(a, b, preferred_element_type=F32, precision=lax.Precision.HIGHEST)


def _delta_kernel(q_ref, k_ref, v_ref, z_ref, gcum_ref, beta_ref, s0_ref, dn_ref, o_ref, s_ref, *, n_chunks):
    c_len = DELTA_CHUNK
    ri = lax.broadcasted_iota(jnp.int32, (c_len, c_len), 0)
    ci = lax.broadcasted_iota(jnp.int32, (c_len, c_len), 1)
    incl = ri >= ci
    strict = ri > ci
    dn = dn_ref[...]

    def chunk(c, S):
        rows = pl.ds(pl.multiple_of(c * c_len, c_len), c_len)
        qc = q_ref[0, rows, :]
        kc = k_ref[0, rows, :]
        vc = v_ref[0, rows, :]
        g_row = jnp.broadcast_to(gcum_ref[0, 0, pl.ds(c, 1), :], (c_len, c_len))
        g_col = g_row.T
        b_col = jnp.broadcast_to(beta_ref[0, 0, pl.ds(c, 1), :], (c_len, c_len)).T
        dec = jnp.where(incl, jnp.exp(jnp.where(incl, g_col - g_row, 0.0)), 0.0)
        kk = lax.dot_general(kc, kc, _NT, preferred_element_type=F32, precision=lax.Precision.HIGHEST)
        a = jnp.where(strict, kk * dec, 0.0) * b_col
        e_g = jnp.exp(g_col)
        u = b_col * (vc - e_g * _dotf(kc, S))
        u = u - _dotf(a, u)
        pw = _dotf(a, a)
        span = 2
        while span < c_len:
            u = u + _dotf(pw, u)
            span *= 2
            if span < c_len:
                pw = _dotf(pw, pw)
        qk = lax.dot_general(qc, kc, _NT, preferred_element_type=F32, precision=lax.Precision.HIGHEST) * dec
        o = e_g * _dotf(qc, S) + _dotf(qk, u)
        g_last = g_row[:, c_len - 1:c_len]
        kw = kc * jnp.exp(g_last - g_col)
        S_new = jnp.exp(g_last) * S + _dotf(kw.T, u)
        z = z_ref[0, rows, :]
        on = o * lax.rsqrt(jnp.mean(o * o, axis=-1, keepdims=True) + EPS) * dn
        o_ref[0, rows, :] = (on * (z / (1.0 + jnp.exp(-z)))).astype(o_ref.dtype)
        return S_new

    s_ref[0, 0] = lax.fori_loop(0, n_chunks, chunk, s0_ref[0, 0])
    tail = o_ref.shape[1] - n_chunks * c_len
    if tail:
        o_ref[0, n_chunks * c_len:, :] = jnp.zeros((tail, o_ref.shape[2]), o_ref.dtype)


def gated_delta(q, k, v, z, gcum, beta, s0, delta_norm, *, n_chunks):
    b, tp, _ = q.shape
    h = DELTA_HEADS
    tok = pl.BlockSpec((1, tp, DELTA_DV), lambda bb, hh: (bb, 0, hh))
    row = pl.BlockSpec((1, 1, n_chunks, DELTA_CHUNK), lambda bb, hh: (bb, hh, 0, 0))
    st = pl.BlockSpec((1, 1, DELTA_DK, DELTA_DV), lambda bb, hh: (bb, hh, 0, 0))
    return pl.pallas_call(
        functools.partial(_delta_kernel, n_chunks=n_chunks), grid=(b, h),
        in_specs=[tok, tok, tok, tok, row, row, st, pl.BlockSpec((1, DELTA_DV), lambda bb, hh: (0, 0))],
        out_specs=[tok, st],
        out_shape=[jax.ShapeDtypeStruct((b, tp, h * DELTA_DV), MXU_DTYPE),
                   jax.ShapeDtypeStruct((b, h, DELTA_DK, DELTA_DV), F32)],
        compiler_params=_cparams("parallel", "parallel"), name="gated_delta",
    )(q, k, v, z, gcum, beta, s0, delta_norm.reshape(1, DELTA_DV).astype(F32))


def _top_values(cur, n):
    vals = []
    for r in range(n):
        m = jnp.max(cur, axis=0, keepdims=True)
        vals.append(m)
        if r + 1 < n:
            cur = jnp.where(cur == m, -jnp.inf, cur)
    return vals


_PEER_PAIRS = [(r, c) for r in range(PEER_TOPK) for c in range(PEER_TOPK) if (r + 1) * (c + 1) <= PEER_TOPK]


def _peer_select_kernel(q_ref, sk_ref, a_ref, b_ref, misc_ref):
    tn = q_ref.shape[1]

    def head(h, carry):
        shifted = []
        tops = []
        for c in range(2):
            st = lax.dot_general(sk_ref[c], q_ref[h * 2 + c], _NT, preferred_element_type=F32)
            vals = _top_values(st, PEER_TOPK)
            shifted.append(st - vals[0])
            tops.append([v - vals[0] for v in vals])
        a_ref[h] = shifted[0]
        b_ref[h] = shifted[1]
        cand = [tops[0][r] + tops[1][c] for r, c in _PEER_PAIRS]
        pad = _round_up(len(cand), 8) - len(cand)
        cand = jnp.concatenate(cand + [jnp.full((pad, tn), -jnp.inf, F32)], axis=0)
        best = _top_values(cand, PEER_TOPK)
        zsum = jnp.zeros_like(best[0])
        for v in best:
            zsum = zsum + jnp.exp(v)
        misc_ref[h] = jnp.concatenate([best[-1], 1.0 / zsum, jnp.zeros((6, tn), F32)], axis=0)
        return carry

    lax.fori_loop(0, PEER_HEADS, head, 0)


def peer_select(q, subkeys, *, tn):
    n = q.shape[1]
    blk = pl.BlockSpec((PEER_HEADS, PEER_NKEYS, tn), lambda i: (0, 0, i))
    return pl.pallas_call(
        _peer_select_kernel, grid=(n // tn,),
        in_specs=[pl.BlockSpec((q.shape[0], tn, q.shape[2]), lambda i: (0, i, 0)),
                  pl.BlockSpec(subkeys.shape, lambda i: (0, 0, 0))],
        out_specs=[blk, blk, pl.BlockSpec((PEER_HEADS, 8, tn), lambda i: (0, 0, i))],
        out_shape=[jax.ShapeDtypeStruct((PEER_HEADS, PEER_NKEYS, n), F32),
                   jax.ShapeDtypeStruct((PEER_HEADS, PEER_NKEYS, n), F32),
                   jax.ShapeDtypeStruct((PEER_HEADS, 8, n), F32)],
        compiler_params=_cparams("parallel"), name="peer_select",
    )(q, subkeys)


def _gelu_tanh(x):
    return 0.5 * x * (1.0 + jnp.tanh(math.sqrt(2.0 / math.pi) * (x + 0.044715 * (x * x * x))))


def _peer_dense_kernel(xn_ref, a_ref, b_ref, misc_ref, u_ref, vt_ref, x_ref, gf_ref, o_ref,
                       acc_ref, aexp_ref, bexp_ref, g_ref):
    e = pl.program_id(1)
    te = u_ref.shape[0]
    nk = PEER_NKEYS

    @pl.when(e == 0)
    def _():
        acc_ref[...] = jnp.zeros_like(acc_ref)
        for h in range(PEER_HEADS):
            aexp_ref[h] = jnp.exp(a_ref[h]) * misc_ref[h, 1:2, :]
            bexp_ref[h] = jnp.exp(b_ref[h])

    ht = lax.dot_general(u_ref[...], xn_ref[...], _NT, preferred_element_type=F32)
    for ii in range(te // nk):
        i = e * (te // nk) + ii
        w = None
        for h in range(PEER_HEADS):
            pair = a_ref[h, pl.ds(i, 1), :] + b_ref[h]
            wh = jnp.where(pair >= misc_ref[h, 0:1, :], aexp_ref[h, pl.ds(i, 1), :] * bexp_ref[h], 0.0)
            w = wh if w is None else w + wh
        rows = slice(ii * nk, (ii + 1) * nk)
        g_ref[rows, :] = (w * _gelu_tanh(ht[rows, :])).astype(g_ref.dtype)
    acc_ref[...] += jnp.dot(vt_ref[...], g_ref[...], preferred_element_type=F32)

    @pl.when(e == pl.num_programs(1) - 1)
    def _():
        y = acc_ref[...].T + x_ref[...]
        ms = jnp.mean(y * y, axis=-1, keepdims=True)
        o_ref[...] = y * lax.rsqrt(ms + EPS) * gf_ref[...]


def peer_dense(xn, a, b, misc, u, vt, x, gain_final, *, tn, te):
    n, d = x.shape
    n_exp = u.shape[0]
    sel = pl.BlockSpec((PEER_HEADS, PEER_NKEYS, tn), lambda i, e: (0, 0, i))
    return pl.pallas_call(
        _peer_dense_kernel, grid=(n // tn, n_exp // te),
        in_specs=[
            pl.BlockSpec((tn, d), lambda i, e: (i, 0)),
            sel, sel,
            pl.BlockSpec((PEER_HEADS, 8, tn), lambda i, e: (0, 0, i)),
            pl.BlockSpec((te, d), lambda i, e: (e, 0)),
            pl.BlockSpec((d, te), lambda i, e: (0, e)),
            pl.BlockSpec((tn, d), lambda i, e: (i, 0)),
            pl.BlockSpec((1, d), lambda i, e: (0, 0)),
        ],
        out_specs=pl.BlockSpec((tn, d), lambda i, e: (i, 0)),
        out_shape=jax.ShapeDtypeStruct((n, d), F32),
        scratch_shapes=[pltpu.VMEM((d, tn), F32),
                        pltpu.VMEM((PEER_HEADS, PEER_NKEYS, tn), F32),
                        pltpu.VMEM((PEER_HEADS, PEER_NKEYS, tn), F32),
                        pltpu.VMEM((te, tn), MXU_DTYPE)],
        compiler_params=_cparams("parallel", "arbitrary"), name="peer_dense",
    )(xn, a, b, misc, u, vt, x, gain_final.reshape(1, d).astype(F32))


def _rope(x, pos):
    rot = x.shape[-1] // 4
    half = rot // 2
    inv = ROPE_THETA ** (-jnp.arange(half, dtype=F32) * 2.0 / rot)
    ang = pos.astype(F32)[:, None] * inv[None, :]
    cos = jnp.cos(ang)[None, :, None, :]
    sin = jnp.sin(ang)[None, :, None, :]
    x1 = x[..., :half]
    x2 = x[..., half:rot]
    return jnp.concatenate([x1 * cos - x2 * sin, x2 * cos + x1 * sin, x[..., rot:]], axis=-1)


def _l2norm(x):
    return x * lax.rsqrt(jnp.sum(x * x, axis=-1, keepdims=True) + EPS)


def _split_projection(p, pos, conv_prev, conv_w, a_log, dt_bias):
    b, t, _ = p.shape
    q = _rope(p[..., COL_Q:COL_K].reshape(b, t, ATTN_HEADS, HEAD_DIM), pos)
    k = _rope(p[..., COL_K:COL_V].reshape(b, t, ATTN_HEADS, HEAD_DIM), pos)
    v = p[..., COL_V:COL_QI].reshape(b, t, ATTN_HEADS, HEAD_DIM)
    qi = _rope(p[..., COL_QI:COL_SMALL].reshape(b, t, IDX_HEADS, IDX_DIM), pos)
    ki = _rope(p[..., COL_SMALL:COL_SMALL + IDX_DIM][:, :, None, :], pos)[:, :, 0, :]
    o = COL_SMALL + IDX_DIM
    wi = p[..., o:o + IDX_HEADS] * (IDX_HEADS * IDX_DIM) ** -0.5
    beta = jax.nn.sigmoid(p[..., o + IDX_HEADS:o + IDX_HEADS + DELTA_HEADS])
    a = p[..., o + IDX_HEADS + DELTA_HEADS:o + IDX_HEADS + 2 * DELTA_HEADS]
    g = -jnp.exp(a_log.astype(F32)) * jax.nn.softplus(a + dt_bias.astype(F32))
    dqkv = p[..., COL_DQKV:COL_Z]
    z = p[..., COL_Z:COL_END]
    xpad = jnp.concatenate([conv_prev.astype(F32), dqkv], axis=1)
    conv = sum(conv_w[i] * xpad[:, i:i + t] for i in range(CONV_WIDTH))
    conv = jax.nn.silu(conv)
    new_conv = xpad[:, t:]
    dq = _l2norm(conv[..., :DELTA_WIDTH].reshape(b, t, DELTA_HEADS, DELTA_DK)) * DELTA_DK ** -0.5
    dk = _l2norm(conv[..., DELTA_WIDTH:2 * DELTA_WIDTH].reshape(b, t, DELTA_HEADS, DELTA_DK))
    dv = conv[..., 2 * DELTA_WIDTH:]
    return (q, k, v, qi, ki, wi), (dq.reshape(b, t, DELTA_WIDTH), dk.reshape(b, t, DELTA_WIDTH), dv, g, beta, z), new_conv


def _pad_rows(x, tp):
    return jnp.pad(x, [(0, 0), (0, tp - x.shape[1])] + [(0, 0)] * (x.ndim - 2))


def _delta_group(dq, dk, dv, g, beta, z, s0, delta_norm):
    b, t, _ = dq.shape
    n_chunks = -(-t // DELTA_CHUNK)
    tp = n_chunks * DELTA_CHUNK

    def rows(x):
        return _pad_rows(x, tp).reshape(b, n_chunks, DELTA_CHUNK, DELTA_HEADS).transpose(0, 3, 1, 2)

    gcum = jnp.cumsum(rows(g), axis=-1)
    o, s = gated_delta(_pad_rows(dq, tp), _pad_rows(dk, tp), _pad_rows(dv, tp), _pad_rows(z, tp),
                       gcum, rows(beta), s0.astype(F32), delta_norm, n_chunks=n_chunks)
    return o[:, :t], s


def _sample_attention(q, k, v, qi, ki, wi, cache_k, cache_v, cache_kidx, page_table, pos, topk):
    db, ds, h, hd = q.shape
    past = page_table.shape[1] * PAGE_SIZE
    bidx = jnp.arange(db)[:, None, None]
    ki_all = jnp.concatenate([cache_kidx[page_table].reshape(db, past, IDX_DIM), ki], axis=1)
    key_pos = jnp.arange(past + ds)
    s = jax.nn.relu(jnp.einsum('bqhd,bsd->bqhs', qi, ki_all))
    scores = jnp.einsum('bqhs,bqh->bqs', s, wi)
    scores = jnp.where((key_pos[None, :] <= pos[:, None])[None], scores, NEG)
    _, idx = lax.top_k(scores, topk)
    valid = idx <= pos[None, :, None]
    in_past = (idx < past)[..., None, None]
    pidx = jnp.minimum(idx, past - 1)
    page = page_table[bidx, pidx // PAGE_SIZE]
    off = pidx % PAGE_SIZE
    nidx = jnp.clip(idx - past, 0, ds - 1)
    ks = jnp.where(in_past, cache_k[page, off], k[bidx, nidx])
    vs = jnp.where(in_past, cache_v[page, off], v[bidx, nidx])
    logits = jnp.einsum('bqhd,bqkhd->bqhk', q, ks) * HEAD_DIM ** -0.5
    logits = jnp.where(valid[:, :, None, :], logits, NEG)
    p = jax.nn.softmax(logits, axis=-1)
    return jnp.einsum('bqhk,bqkhd->bqhd', p, vs).reshape(db, ds, h * hd)


def _layer(l, x, dims, cache_k, cache_v, cache_kidx, state_conv, state_ssm, page_table,
           norm_mix, w_in, conv_w, a_log, dt_bias, delta_norm, w_out, norm_ffn,
           peer_wq, peer_subkeys, peer_u, peer_v, norm_out):
    b, t, db, ds = dims
    n_pad, d = x.shape
    n_p, n_s = b * t, db * ds
    past = page_table.shape[1] * PAGE_SIZE

    w = w_in[l]
    o_wi = 3 * ATTN_WIDTH + IDX_HEADS * IDX_DIM + IDX_DIM + IDX_HEADS
    o_z = o_wi + 3 * DELTA_WIDTH
    o_b = o_z + DELTA_WIDTH
    w_perm = jnp.concatenate(
        [w[:, :o_wi], w[:, o_b:o_b + 2 * DELTA_HEADS],
         jnp.zeros((d, COL_DQKV - (o_wi + 2 * DELTA_HEADS)), w.dtype),
         w[:, o_wi:o_b], jnp.zeros((d, PROJ_PAD - COL_END), w.dtype)], axis=1).astype(MXU_DTYPE)
    p = norm_matmul(x, norm_mix[l], w_perm, tm=_token_tile(n_pad, 1280), tn=1024, out_dtype=F32, return_normed=False)

    pos_p = jnp.arange(t)
    (q, k_p, v_p, qi, ki_p, wi), (dq, dk, dv, g, beta, z), conv_p = _split_projection(
        p[:n_p].reshape(b, t, PROJ_PAD), pos_p, jnp.zeros((b, CONV_WIDTH - 1, 3 * DELTA_WIDTH), F32),
        conv_w[l], a_log[l], dt_bias[l])
    tq = 256
    tp = _round_up(t, tq)
    topk_p = min(TOPK_MAX, (t - N_META) // 4)
    attn_p = attn_prompt(
        _pad_rows(q.reshape(b, t, ATTN_WIDTH), tp).astype(MXU_DTYPE),
        _pad_rows(k_p.reshape(b, t, ATTN_WIDTH), tp).astype(MXU_DTYPE),
        _pad_rows(v_p.reshape(b, t, ATTN_WIDTH), tp).astype(MXU_DTYPE),
        _pad_rows(qi, tp).transpose(0, 2, 1, 3).astype(MXU_DTYPE),
        _pad_rows(ki_p, tp).astype(MXU_DTYPE), _pad_rows(wi, tp), tq=tq, topk=topk_p)[:, :t]
    gated_p, ssm_p = _delta_group(dq, dk, dv, g, beta, z,
                                  jnp.zeros((b, DELTA_HEADS, DELTA_DK, DELTA_DV), F32), delta_norm[l])

    pos_s = past + jnp.arange(ds)
    (q, k_s, v_s, qi, ki_s, wi), (dq, dk, dv, g, beta, z), conv_s = _split_projection(
        p[n_p:n_p + n_s].reshape(db, ds, PROJ_PAD), pos_s, state_conv[l], conv_w[l], a_log[l], dt_bias[l])
    topk_s = min(TOPK_MAX, (past + ds) // 4)
    attn_s = _sample_attention(q, k_s, v_s, qi, ki_s, wi, cache_k[l], cache_v[l], cache_kidx[l],
                               page_table, pos_s, topk_s).astype(MXU_DTYPE)
    gated_s, ssm_s = _delta_group(dq, dk, dv, g, beta, z, state_ssm[l], delta_norm[l])

    def flat(xp_, xs_):
        return jnp.concatenate([xp_.reshape(n_p, -1), xs_.reshape(n_s, -1),
                                jnp.zeros((n_pad - n_p - n_s, xp_.shape[-1]), xp_.dtype)], axis=0)

    wo = w_out[l].astype(MXU_DTYPE)
    x1 = matmul2_residual(flat(attn_p, attn_s), flat(gated_p, gated_s), wo[:ATTN_WIDTH], wo[ATTN_WIDTH:], x,
                          tm=_token_tile(n_pad, 1280), tn=1024 if d % 1024 == 0 else d)

    wq = peer_wq[l].astype(MXU_DTYPE)
    qp, xn = norm_matmul(x1, norm_ffn[l], wq, tm=_token_tile(n_pad, 1280), tn=wq.shape[1], out_dtype=MXU_DTYPE,
                         return_normed=True)
    qp = qp.reshape(n_pad, 2 * PEER_HEADS, PEER_QDIM // 2).transpose(1, 0, 2)
    a, bb, misc = peer_select(qp,peer_subkeys[l].astype(MXU_DTYPE), tn=TOKEN_BLOCK)
    y = peer_dense(xn, a, bb, misc, peer_u[l].astype(MXU_DTYPE), peer_v[l].T.astype(MXU_DTYPE), x1, norm_out,
                   tn=TOKEN_BLOCK, te=512)
    caches_p = (k_p, v_p, ki_p, conv_p, ssm_p)
    caches_s = (k_s, v_s, ki_s, conv_s, ssm_s)
    return y, caches_p, caches_s


def _token_tile(n, want):
    best = 128
    for m in range(128, want + 1, 128):
        if n % m == 0:
            best = m
    return best


def kernel(x_prompt, x_sample, cache_k, cache_v, cache_kidx, state_conv, state_ssm, page_table, meta, norm_mix,
           w_in, conv_w, a_log, dt_bias, delta_norm, w_out, norm_ffn, peer_wq, peer_subkeys, peer_u, peer_v,
           norm_final):
    b, s, d = x_prompt.shape
    db, ds, _ = x_sample.shape
    t = s + N_META
    depth = w_in.shape[0]
    assert depth == 1, "the fused PEER + final-norm epilogue assumes a single layer"
    n_p, n_s = b * t, db * ds
    n_pad = _round_up(n_p + n_s, TOKEN_BLOCK)
    xp = jnp.concatenate([jnp.broadcast_to(meta[None].astype(F32), (b, N_META, d)), x_prompt], axis=1)
    x = jnp.concatenate([xp.reshape(n_p, d), x_sample.reshape(n_s, d), jnp.zeros((n_pad - n_p - n_s, d), F32)], axis=0)
    y, cp, cs = _layer(0, x, (b, t, db, ds), cache_k, cache_v, cache_kidx, state_conv, state_ssm, page_table,
                       norm_mix, w_in, conv_w, a_log, dt_bias, delta_norm, w_out, norm_ffn,
                       peer_wq, peer_subkeys, peer_u, peer_v, norm_final)
    y_prompt = y[:n_p].reshape(b, t, d)[:, N_META:]
    y_sample = y[n_p:n_p + n_s].reshape(db, ds, d)
    return (y_prompt, y_sample) + tuple(c[None] for c in cp) + tuple(c[None] for c in cs)
```

```python
import functools
import math

import jax
import jax.numpy as jnp
from jax import lax
from jax.experimental import pallas as pl
from jax.experimental.pallas import tpu as pltpu

F32 = jnp.float32
MXU_DTYPE = jnp.bfloat16

N_META = 16
HEAD_DIM = 128
ATTN_HEADS = 8
ATTN_WIDTH = ATTN_HEADS * HEAD_DIM
IDX_HEADS = 8
IDX_DIM = 64
TOPK_MAX = 256
ROPE_THETA = 500000.0
DELTA_DK = 128
DELTA_DV = 128
DELTA_HEADS = 8
DELTA_WIDTH = DELTA_HEADS * DELTA_DV
CONV_WIDTH = 4
PEER_HEADS = 8
PEER_NKEYS = 128
PEER_QDIM = 256
PEER_TOPK = 16
PAGE_SIZE = 128
EPS = 1e-6
NEG = -1e30

COL_Q = 0
COL_K = COL_Q + ATTN_WIDTH
COL_V = COL_K + ATTN_WIDTH
COL_QI = COL_V + ATTN_WIDTH
COL_SMALL = COL_QI + IDX_HEADS * IDX_DIM
COL_DQKV = COL_SMALL + 128
COL_Z = COL_DQKV + 3 * DELTA_WIDTH
COL_END = COL_Z + DELTA_WIDTH
PROJ_PAD = 8192

LANES = 128
TOKEN_BLOCK = 512
ATTN_QBLOCK = 256
DELTA_CHUNK = 128
DELTA_HEADS_PER_STEP = 4
SCORE_PAGES_PER_STEP = 8
ATTEND_PAGES_PER_STEP = 4
PEER_EXPERT_BLOCK = 1024
VMEM_LIMIT = 56 * 1024 * 1024

_NT = (((1,), (1,)), ((), ()))


def _round_up(x, m):
    return (x + m - 1) // m * m


def _cparams(*sem):
    return pltpu.CompilerParams(dimension_semantics=sem, vmem_limit_bytes=VMEM_LIMIT)


def _norm_matmul_kernel(x_ref, g_ref, w_ref, o_ref, xn_ref):
    @pl.when(pl.program_id(1) == 0)
    def _():
        x = x_ref[...]
        ms = jnp.mean(x * x, axis=-1, keepdims=True)
        xn_ref[...] = (x * lax.rsqrt(ms + EPS) * g_ref[...]).astype(xn_ref.dtype)

    o_ref[...] = jnp.dot(xn_ref[...], w_ref[...], preferred_element_type=F32).astype(o_ref.dtype)


def norm_matmul(x, gain, w, *, tm, tn, out_dtype, return_normed):
    n, d = x.shape
    width = w.shape[1]
    grid = (n // tm, width // tn)
    in_specs = [
        pl.BlockSpec((tm, d), lambda i, j: (i, 0)),
        pl.BlockSpec((1, d), lambda i, j: (0, 0)),
        pl.BlockSpec((d, tn), lambda i, j: (0, j)),
    ]
    o_spec = pl.BlockSpec((tm, tn), lambda i, j: (i, j))
    o_shape = jax.ShapeDtypeStruct((n, width), out_dtype)
    xn_spec = pl.BlockSpec((tm, d), lambda i, j: (i, 0))
    if return_normed:
        return pl.pallas_call(
            _norm_matmul_kernel, grid=grid, in_specs=in_specs,
            out_specs=[o_spec, xn_spec],
            out_shape=[o_shape, jax.ShapeDtypeStruct((n, d), MXU_DTYPE)],
            compiler_params=_cparams("parallel", "arbitrary"), name="norm_matmul_xn",
        )(x, gain.reshape(1, d).astype(F32), w)
    return pl.pallas_call(
        _norm_matmul_kernel, grid=grid, in_specs=in_specs, out_specs=o_spec, out_shape=o_shape,
        scratch_shapes=[pltpu.VMEM((tm, d), MXU_DTYPE)],
        compiler_params=_cparams("parallel", "arbitrary"), name="norm_matmul",
    )(x, gain.reshape(1, d).astype(F32), w)


def _matmul2_kernel(a1_ref, a2_ref, w1_ref, w2_ref, r_ref, o_ref):
    acc = jnp.dot(a1_ref[...], w1_ref[...], preferred_element_type=F32)
    acc = acc + jnp.dot(a2_ref[...], w2_ref[...], preferred_element_type=F32)
    o_ref[...] = acc + r_ref[...]


def matmul2_residual(a1, a2, w1, w2, r, *, tm, tn):
    n, k1 = a1.shape
    k2 = a2.shape[1]
    d = w1.shape[1]
    return pl.pallas_call(
        _matmul2_kernel, grid=(n // tm, d // tn),
        in_specs=[
            pl.BlockSpec((tm, k1), lambda i, j: (i, 0)),
            pl.BlockSpec((tm, k2), lambda i, j: (i, 0)),
            pl.BlockSpec((k1, tn), lambda i, j: (0, j)),
            pl.BlockSpec((k2, tn), lambda i, j: (0, j)),
            pl.BlockSpec((tm, tn), lambda i, j: (i, j)),
        ],
        out_specs=pl.BlockSpec((tm, tn), lambda i, j: (i, j)),
        out_shape=jax.ShapeDtypeStruct((n, d), F32),
        compiler_params=_cparams("parallel", "arbitrary"), name="out_proj",
    )(a1, a2, w1, w2, r)


def _order_key(s):
    s = jnp.where(s == 0.0, 0.0, s)
    key = lax.bitcast_convert_type(s, jnp.int32)
    return jnp.where(key < 0, key ^ jnp.int32(0x7FFFFFFF), key)


def _count_lanes(mask):
    return jnp.sum(jnp.where(mask, 1.0, 0.0), axis=-1, keepdims=True)


def _count_pages_lanes(mask):
    per_query = jnp.sum(jnp.where(mask, 1.0, 0.0), axis=0, keepdims=True)
    return jnp.sum(per_query, axis=2, keepdims=True)


def _topk_threshold(key, col, topk, idx_bits, count):
    kf = float(topk)
    nonneg = count(key >= 0) >= kf
    base = jnp.where(nonneg, jnp.int32(0), jnp.int32(-2 ** 31))

    def value_bit(it, base):
        cand = base | jnp.left_shift(jnp.int32(1), 30 - it)
        return jnp.where(count(key >= cand) >= kf, cand, base)

    thr = lax.fori_loop(0, 31, value_bit, base)
    tied = key == thr
    need = kf - count(key > thr)

    def index_bit(it, j):
        cand = j | jnp.left_shift(jnp.int32(1), idx_bits - 1 - it)
        return jnp.where(count(tied & (col < cand)) < need, cand, j)

    j = lax.fori_loop(0, idx_bits, index_bit, jnp.zeros_like(thr))
    return thr, j


def _attn_prompt_kernel(q_ref, k_ref, v_ref, qi_ref, ki_ref, wi_ref, o_ref, *, q_start, topk, idx_bits):
    tq = q_ref.shape[1]
    kl = k_ref.shape[1]
    ki = ki_ref[0]
    wi = wi_ref[0]
    s = jnp.zeros((tq, kl), F32)
    for h in range(IDX_HEADS):
        d = lax.dot_general(qi_ref[0, h], ki, _NT, preferred_element_type=F32)
        s = s + jnp.maximum(d, 0.0) * wi[:, h:h + 1]
    row = q_start + lax.broadcasted_iota(jnp.int32, (tq, kl), 0)
    col = lax.broadcasted_iota(jnp.int32, (tq, kl), 1)
    causal = col <= row
    key = _order_key(jnp.where(causal, s, NEG))
    thr, j = _topk_threshold(key, col, topk, idx_bits, _count_lanes)
    sel = (key > thr) | ((key == thr) & (col <= j))
    bias = jnp.where(sel & causal, 0.0, NEG)
    scale = HEAD_DIM ** -0.5
    for h in range(ATTN_HEADS):
        hs = slice(h * HEAD_DIM, (h + 1) * HEAD_DIM)
        logits = lax.dot_general(q_ref[0, :, hs], k_ref[0, :, hs], _NT, preferred_element_type=F32) * scale + bias
        m = jnp.max(logits, axis=-1, keepdims=True)
        p = jnp.exp(logits - m)
        l = jnp.sum(p, axis=-1, keepdims=True)
        o = jnp.dot(p.astype(v_ref.dtype), v_ref[0, :, hs], preferred_element_type=F32)
        o_ref[0, :, hs] = (o / l).astype(o_ref.dtype)


def attn_prompt(q, k, v, qi, ki, wi, *, tq, topk):
    b, tp, w = q.shape
    outs = []
    for i in range(tp // tq):
        kl = (i + 1) * tq
        kern = functools.partial(_attn_prompt_kernel, q_start=i * tq, topk=topk,
                                 idx_bits=max(1, (kl - 1).bit_length()))
        outs.append(pl.pallas_call(
            kern, grid=(b,),
            in_specs=[
                pl.BlockSpec((1, tq, w), lambda bb, i=i: (bb, i, 0)),
                pl.BlockSpec((1, kl, w), lambda bb: (bb, 0, 0)),
                pl.BlockSpec((1, kl, w), lambda bb: (bb, 0, 0)),
                pl.BlockSpec((1, IDX_HEADS, tq, IDX_DIM), lambda bb, i=i: (bb, 0, i, 0)),
                pl.BlockSpec((1, kl, IDX_DIM), lambda bb: (bb, 0, 0)),
                pl.BlockSpec((1, tq, IDX_HEADS), lambda bb, i=i: (bb, i, 0)),
            ],
            out_specs=pl.BlockSpec((1, tq, w), lambda bb: (bb, 0, 0)),
            out_shape=jax.ShapeDtypeStruct((b, tq, w), MXU_DTYPE),
            compiler_params=_cparams("parallel"), name=f"attn_prompt_q{i}",
        )(q, k, v, qi, ki, wi))
    return jnp.concatenate(outs, axis=1)


def _page_scores(qi, wcol, kpage, ds):
    d = lax.dot_general(qi, kpage.astype(qi.dtype), _NT, preferred_element_type=F32)
    r = jnp.maximum(d, 0.0) * wcol
    return jnp.sum(r.reshape(ds, IDX_HEADS, r.shape[-1]), axis=1)


def _sample_scores_kernel(pt_ref, qi_ref, wi_ref, knew_ref, *rest, n_pages_step, ds):
    page_refs = rest[:n_pages_step]
    past_ref, new_ref = rest[n_pages_step:]
    qi = qi_ref[0]
    wcol = wi_ref[0]
    for g in range(n_pages_step):
        past_ref[0, g] = _page_scores(qi, wcol, page_refs[g][0], ds)
    s_new = _page_scores(qi, wcol, knew_ref[0], ds)
    qrow = lax.broadcasted_iota(jnp.int32, s_new.shape, 0)
    kcol = lax.broadcasted_iota(jnp.int32, s_new.shape, 1)
    new_ref[0] = jnp.where(kcol <= qrow, s_new, NEG)


def sample_scores(qi, wi, ki_new_page, cache_kidx, page_table, *, ds):
    db, n_pages = page_table.shape
    g = math.gcd(SCORE_PAGES_PER_STEP, n_pages)
    rows = ds * IDX_HEADS
    page_specs = [pl.BlockSpec((1, PAGE_SIZE, IDX_DIM), lambda b, p, pt, gg=gg: (pt[b, p * g + gg], 0, 0))
                  for gg in range(g)]
    grid_spec = pltpu.PrefetchScalarGridSpec(
        num_scalar_prefetch=1, grid=(db, n_pages // g),
        in_specs=[pl.BlockSpec((1, rows, IDX_DIM), lambda b, p, pt: (b, 0, 0)),
                  pl.BlockSpec((1, rows, 1), lambda b, p, pt: (b, 0, 0)),
                  pl.BlockSpec((1, PAGE_SIZE, IDX_DIM), lambda b, p, pt: (b, 0, 0))] + page_specs,
        out_specs=[pl.BlockSpec((1, g, ds, PAGE_SIZE), lambda b, p, pt: (b, p, 0, 0)),
                   pl.BlockSpec((1, ds, PAGE_SIZE), lambda b, p, pt: (b, 0, 0))])
    return pl.pallas_call(
        functools.partial(_sample_scores_kernel, n_pages_step=g, ds=ds), grid_spec=grid_spec,
        out_shape=[jax.ShapeDtypeStruct((db, n_pages, ds, PAGE_SIZE), F32),
                   jax.ShapeDtypeStruct((db, ds, PAGE_SIZE), F32)],
        compiler_params=_cparams("parallel", "arbitrary"), name="sample_scores",
    )(page_table, qi, wi, ki_new_page, *([cache_kidx] * g))


def _sample_attend_kernel(pt_ref, sc_ref, scn_ref, q_ref, kn_ref, vn_ref, *rest, n_pages_step, ds, topk, idx_bits):
    k_refs = rest[:n_pages_step]
    v_refs = rest[n_pages_step:2 * n_pages_step]
    o_ref, thr_ref, j_ref, m_ref, l_ref, acc_ref = rest[2 * n_pages_step:]
    p = pl.program_id(1)
    n_pages = sc_ref.shape[1]
    rows = ds * ATTN_HEADS
    scale = HEAD_DIM ** -0.5

    @pl.when(p == 0)
    def _():
        key = _order_key(jnp.concatenate([sc_ref[0], scn_ref[0][None]], axis=0))
        col = (lax.broadcasted_iota(jnp.int32, key.shape, 0) * PAGE_SIZE
               + lax.broadcasted_iota(jnp.int32, key.shape, 2))
        thr, j = _topk_threshold(key, col, topk, idx_bits, _count_pages_lanes)
        thr_ref[...] = thr[0]
        j_ref[...] = j[0]
        m_ref[...] = jnp.full_like(m_ref, NEG)
        l_ref[...] = jnp.zeros_like(l_ref)
        acc_ref[...] = jnp.zeros_like(acc_ref)

    def attend(scores, page_idx, kpage, vpage):
        key = _order_key(scores)
        col = page_idx * PAGE_SIZE + lax.broadcasted_iota(jnp.int32, key.shape, 1)
        sel = ((key > thr_ref[...]) | ((key == thr_ref[...]) & (col <= j_ref[...]))) & (scores > 0.5 * NEG)
        bias = jnp.where(sel, 0.0, NEG)
        bias = jnp.broadcast_to(bias[:, None, :], (ds, ATTN_HEADS, PAGE_SIZE)).reshape(rows, PAGE_SIZE)
        logits = lax.dot_general(q_ref[0], kpage.astype(q_ref.dtype), _NT, preferred_element_type=F32) * scale + bias
        m_new = jnp.maximum(m_ref[...], jnp.max(logits, axis=-1, keepdims=True))
        alpha = jnp.exp(m_ref[...] - m_new)
        pr = jnp.exp(logits - m_new)
        l_ref[...] = alpha * l_ref[...] + jnp.sum(pr, axis=-1, keepdims=True)
        acc_ref[...] = alpha * acc_ref[...] + jnp.dot(pr.astype(q_ref.dtype), vpage.astype(q_ref.dtype),
                                                      preferred_element_type=F32)
        m_ref[...] = m_new

    for g in range(n_pages_step):
        page_idx = p * n_pages_step + g
        attend(sc_ref[0, page_idx], page_idx, k_refs[g][0], v_refs[g][0])

    @pl.when(p == pl.num_programs(1) - 1)
    def _():
        attend(scn_ref[0], n_pages, kn_ref[0], vn_ref[0])
        o_ref[0] = acc_ref[...] / l_ref[...]


def sample_attend(scores, scores_new, q_exp, k_new_page, v_new_page, cache_k, cache_v, page_table, *, ds, topk):
    db, n_pages = page_table.shape
    g = math.gcd(ATTEND_PAGES_PER_STEP, n_pages)
    rows = ds * ATTN_HEADS
    w = ATTN_WIDTH
    idx_bits = max(1, ((n_pages + 1) * PAGE_SIZE - 1).bit_length())

    def page_spec(gg):
        return pl.BlockSpec((1, PAGE_SIZE, w), lambda b, p, pt, gg=gg: (pt[b, p * g + gg], 0, 0))

    grid_spec = pltpu.PrefetchScalarGridSpec(
        num_scalar_prefetch=1, grid=(db, n_pages // g),
        in_specs=[pl.BlockSpec((1, n_pages, ds, PAGE_SIZE), lambda b, p, pt: (b, 0, 0, 0)),
                  pl.BlockSpec((1, ds, PAGE_SIZE), lambda b, p, pt: (b, 0, 0)),
                  pl.BlockSpec((1, rows, w), lambda b, p, pt: (b, 0, 0)),
                  pl.BlockSpec((1, PAGE_SIZE, w), lambda b, p, pt: (b, 0, 0)),
                  pl.BlockSpec((1, PAGE_SIZE, w), lambda b, p, pt: (b, 0, 0))]
        + [page_spec(gg) for gg in range(g)] * 2,
        out_specs=pl.BlockSpec((1, rows, w), lambda b, p, pt: (b, 0, 0)),
        scratch_shapes=[pltpu.VMEM((ds, 1), jnp.int32), pltpu.VMEM((ds, 1), jnp.int32),
                        pltpu.VMEM((rows, 1), F32), pltpu.VMEM((rows, 1), F32), pltpu.VMEM((rows, w), F32)])
    kern = functools.partial(_sample_attend_kernel, n_pages_step=g, ds=ds, topk=topk, idx_bits=idx_bits)
    return pl.pallas_call(
        kern, grid_spec=grid_spec, out_shape=jax.ShapeDtypeStruct((db, rows, w), F32),
        compiler_params=_cparams("parallel", "arbitrary"), name="sample_attend",
    )(page_table, scores, scores_new, q_exp, k_new_page, v_new_page, *([cache_k] * g), *([cache_v] * g))


def _split_hi_lo(x):
    hi = x.astype(MXU_DTYPE)
    return hi, (x - hi.astype(F32)).astype(MXU_DTYPE)


def _dot1(a, b, dims=None):
    if dims is None:
        dims = (((1,), (0,)), ((), ()))
    return lax.dot_general(a.astype(MXU_DTYPE), b.astype(MXU_DTYPE), dims, preferred_element_type=F32)


def _dot3(a, b, dims=None):
    ah, al = _split_hi_lo(a)
    bh, bl = _split_hi_lo(b)
    if dims is None:
        dims = (((1,), (0,)), ((), ()))
    d = functools.partial(lax.dot_general, dimension_numbers=dims, preferred_element_type=F32)
    return d(ah, bh) + (d(ah, bl) + d(al, bh))


def _delta_kernel(q_ref, k_ref, v_ref, z_ref, gcum_ref, beta_ref, s0_ref, dn_ref, o_ref, s_ref, t_scr, qk_scr,
                  *, n_chunks):
    c_len = DELTA_CHUNK
    n_h = s0_ref.shape[1]
    ri = lax.broadcasted_iota(jnp.int32, (c_len, c_len), 0)
    ci = lax.broadcasted_iota(jnp.int32, (c_len, c_len), 1)
    incl = ri >= ci
    strict = ri > ci
    eye = jnp.where(ri == ci, 1.0, 0.0)
    dn = dn_ref[...]

    def chunk_rows(c):
        return pl.ds(pl.multiple_of(c * c_len, c_len), c_len)

    def decay_terms(c, hh):
        g_row = jnp.broadcast_to(gcum_ref[0, hh, pl.ds(c, 1), :], (c_len, c_len))
        b_col = jnp.broadcast_to(beta_ref[0, hh, pl.ds(c, 1), :], (c_len, c_len)).T
        return g_row, g_row.T, b_col

    heads = range(n_h)
    head_cols = [slice(hh * DELTA_DK, (hh + 1) * DELTA_DK) for hh in heads]

    def build(c, carry):
        rows = chunk_rows(c)
        qc = [q_ref[0, rows, cs] for cs in head_cols]
        kc = [k_ref[0, rows, cs] for cs in head_cols]
        terms = [decay_terms(c, hh) for hh in heads]
        dec = [jnp.where(incl, jnp.exp(jnp.where(incl, g_col - g_row, 0.0)), 0.0) for g_row, g_col, _ in terms]
        kk = [_dot3(kc[hh], kc[hh], _NT) for hh in heads]
        a = [jnp.where(strict, kk[hh] * dec[hh], 0.0) * terms[hh][2] for hh in heads]
        t = [eye - a[hh] for hh in heads]
        pw = [_dot1(a[hh], a[hh]) for hh in heads]
        span = 2
        while span < c_len:
            t = [t[hh] + _dot1(t[hh], pw[hh]) for hh in heads]
            span *= 2
            if span < c_len:
                pw = [_dot1(pw[hh], pw[hh]) for hh in heads]
        at = [_dot3(a[hh], t[hh]) for hh in heads]
        corr = [_dot3(t[hh], eye - t[hh] - at[hh]) for hh in heads]
        qk = [_dot1(qc[hh], kc[hh], _NT) for hh in heads]
        for hh in heads:
            t_scr[hh, c] = t[hh] + corr[hh]
            qk_scr[hh, c] = qk[hh] * dec[hh]
        return carry

    lax.fori_loop(0, n_chunks, build, 0)

    def step(c, states):
        rows = chunk_rows(c)
        qc = [q_ref[0, rows, cs] for cs in head_cols]
        kc = [k_ref[0, rows, cs] for cs in head_cols]
        terms = [decay_terms(c, hh) for hh in heads]
        e_g = [jnp.exp(g_col) for _, g_col, _ in terms]
        ks = [_dot1(kc[hh], states[hh]) for hh in heads]
        qs = [_dot1(qc[hh], states[hh]) for hh in heads]
        u = [_dot3(t_scr[hh, c], terms[hh][2] * (v_ref[0, rows, head_cols[hh]] - e_g[hh] * ks[hh])) for hh in heads]
        qku = [_dot1(qk_scr[hh, c], u[hh]) for hh in heads]
        g_last = [g_row[:, c_len - 1:c_len] for g_row, _, _ in terms]
        kw = [kc[hh] * jnp.exp(g_last[hh] - terms[hh][1]) for hh in heads]
        upd = [_dot1(kw[hh].T, u[hh]) for hh in heads]
        for hh in heads:
            o = e_g[hh] * qs[hh] + qku[hh]
            z = z_ref[0, rows, head_cols[hh]]
            on = o * lax.rsqrt(jnp.mean(o * o, axis=-1, keepdims=True) + EPS) * dn
            o_ref[0, rows, head_cols[hh]] = (on * (z / (1.0 + jnp.exp(-z)))).astype(o_ref.dtype)
        return tuple(jnp.exp(g_last[hh]) * states[hh] + upd[hh] for hh in heads)

    final = lax.fori_loop(0, n_chunks, step, tuple(s0_ref[0, hh] for hh in range(n_h)))
    for hh in range(n_h):
        s_ref[0, hh] = final[hh]


def gated_delta(q, k, v, z, gcum, beta, s0, delta_norm, *, n_chunks):
    b, tp, _ = q.shape
    h = DELTA_HEADS
    n_h = DELTA_HEADS_PER_STEP
    tok = pl.BlockSpec((1, tp, n_h * DELTA_DV), lambda bb, hh: (bb, 0, hh))
    row = pl.BlockSpec((1, n_h, n_chunks, DELTA_CHUNK), lambda bb, hh: (bb, hh, 0, 0))
    st = pl.BlockSpec((1, n_h, DELTA_DK, DELTA_DV), lambda bb, hh: (bb, hh, 0, 0))
    mat = pltpu.VMEM((n_h, n_chunks, DELTA_CHUNK, DELTA_CHUNK), F32)
    return pl.pallas_call(
        functools.partial(_delta_kernel, n_chunks=n_chunks), grid=(b, h // n_h),
        in_specs=[tok, tok, tok, tok, row, row, st, pl.BlockSpec((1, DELTA_DV), lambda bb, hh: (0, 0))],
        out_specs=[tok, st],
        out_shape=[jax.ShapeDtypeStruct((b, tp, h * DELTA_DV), MXU_DTYPE),
                   jax.ShapeDtypeStruct((b, h, DELTA_DK, DELTA_DV), F32)],
        scratch_shapes=[mat, mat],
        compiler_params=_cparams("parallel", "parallel"), name="gated_delta",
    )(q, k, v, z, gcum, beta, s0, delta_norm.reshape(1, DELTA_DV).astype(F32))


def _top_values(cur, n):
    vals = []
    for r in range(n):
        m = jnp.max(cur, axis=0, keepdims=True)
        vals.append(m)
        if r + 1 < n:
            cur = jnp.where(cur == m, -jnp.inf, cur)
    return vals


_PEER_RANKS = PEER_TOPK + 1
_PEER_PAIRS = [(r, c) for r in range(_PEER_RANKS) for c in range(_PEER_RANKS) if (r + 1) * (c + 1) <= _PEER_RANKS]


def _peer_select_kernel(q_ref, sk_ref, a_ref, b_ref, misc_ref):
    tn = q_ref.shape[1]

    def head(h, carry):
        shifted = []
        tops = []
        for c in range(2):
            st = lax.dot_general(sk_ref[c], q_ref[h * 2 + c], _NT, preferred_element_type=F32)
            vals = _top_values(st, _PEER_RANKS)
            shifted.append(st - vals[0])
            tops.append([v - vals[0] for v in vals])
        a_ref[h] = shifted[0]
        b_ref[h] = shifted[1]
        cand = [tops[0][r] + tops[1][c] for r, c in _PEER_PAIRS]
        pad = _round_up(len(cand), 8) - len(cand)
        cand = jnp.concatenate(cand + [jnp.full((pad, tn), -jnp.inf, F32)], axis=0)
        best = _top_values(cand, _PEER_RANKS)
        zsum = jnp.zeros_like(best[0])
        for v in best[:PEER_TOPK]:
            zsum = zsum + jnp.exp(v)
        thr = 0.5 * (best[PEER_TOPK - 1] + best[PEER_TOPK])
        misc_ref[h] = jnp.concatenate([thr, 1.0 / zsum, jnp.zeros((6, tn), F32)], axis=0)
        return carry

    lax.fori_loop(0, PEER_HEADS, head, 0)


def peer_select(q, subkeys, *, tn):
    n = q.shape[1]
    blk = pl.BlockSpec((PEER_HEADS, PEER_NKEYS, tn), lambda i: (0, 0, i))
    return pl.pallas_call(
        _peer_select_kernel, grid=(n // tn,),
        in_specs=[pl.BlockSpec((q.shape[0], tn, q.shape[2]), lambda i: (0, i, 0)),
                  pl.BlockSpec(subkeys.shape, lambda i: (0, 0, 0))],
        out_specs=[blk, blk, pl.BlockSpec((PEER_HEADS, 8, tn), lambda i: (0, 0, i))],
        out_shape=[jax.ShapeDtypeStruct((PEER_HEADS, PEER_NKEYS, n), F32),
                   jax.ShapeDtypeStruct((PEER_HEADS, PEER_NKEYS, n), F32),
                   jax.ShapeDtypeStruct((PEER_HEADS, 8, n), F32)],
        compiler_params=_cparams("parallel"), name="peer_select",
    )(q, subkeys)


def _gelu_tanh(x):
    return 0.5 * x * (1.0 + jnp.tanh(math.sqrt(2.0 / math.pi) * (x + 0.044715 * (x * x * x))))


def _peer_dense_kernel(xn_ref, a_ref, b_ref, misc_ref, u_ref, vt_ref, x_ref, gf_ref, o_ref,
                       acc_ref, bexp_ref, g_ref, row_ref, ht_ref):
    e = pl.program_id(1)
    te = u_ref.shape[0]
    tn = xn_ref.shape[0]
    nk = PEER_NKEYS

    @pl.when(e == 0)
    def _():
        acc_ref[...] = jnp.zeros_like(acc_ref)
        for h in range(PEER_HEADS):
            bexp_ref[h] = jnp.exp(b_ref[h])

    n_i = te // nk
    ht_ref[...] = lax.dot_general(u_ref[...], xn_ref[...], _NT, preferred_element_type=F32)

    thr_all = jnp.concatenate([misc_ref[h, 0:1, :] for h in range(PEER_HEADS)], axis=0)
    invz_all = jnp.concatenate([misc_ref[h, 1:2, :] for h in range(PEER_HEADS)], axis=0)
    for ii in range(n_i):
        a_rows = jnp.concatenate([a_ref[h, pl.ds(e * n_i + ii, 1), :] for h in range(PEER_HEADS)], axis=0)
        row_ref[0, ii] = thr_all - a_rows
        row_ref[1, ii] = jnp.exp(a_rows) * invz_all

    key_tile = 32
    for tb in range(tn // LANES):
        lanes = slice(tb * LANES, (tb + 1) * LANES)
        for kt in range(nk // key_tile):
            keys = slice(kt * key_tile, (kt + 1) * key_tile)
            w = [None] * n_i
            for h in range(PEER_HEADS):
                b_tile = b_ref[h, keys, lanes]
                bexp_tile = bexp_ref[h, keys, lanes]
                for ii in range(n_i):
                    wh = (jnp.where(b_tile >= row_ref[0, ii, h:h + 1, lanes], bexp_tile, 0.0)
                          * row_ref[1, ii, h:h + 1, lanes])
                    w[ii] = wh if w[ii] is None else w[ii] + wh
            for ii in range(n_i):
                rows = slice(ii * nk + kt * key_tile, ii * nk + (kt + 1) * key_tile)
                g_ref[rows, lanes] = (w[ii] * _gelu_tanh(ht_ref[rows, lanes])).astype(g_ref.dtype)
    acc_ref[...] += jnp.dot(vt_ref[...], g_ref[...], preferred_element_type=F32)

    @pl.when(e == pl.num_programs(1) - 1)
    def _():
        y = acc_ref[...].T + x_ref[...]
        ms = jnp.mean(y * y, axis=-1, keepdims=True)
        o_ref[...] = y * lax.rsqrt(ms + EPS) * gf_ref[...]


def peer_dense(xn, a, b, misc, u, vt, x, gain_final, *, tn, te):
    n, d = x.shape
    n_blocks = u.shape[0] // te
    once = pl.Buffered(1)
    sel = pl.BlockSpec((PEER_HEADS, PEER_NKEYS, tn), lambda i, e: (0, 0, i), pipeline_mode=once)
    return pl.pallas_call(
        _peer_dense_kernel, grid=(n // tn, n_blocks),
        in_specs=[
            pl.BlockSpec((tn, d), lambda i, e: (i, 0), pipeline_mode=once),
            sel, sel,
            pl.BlockSpec((PEER_HEADS, 8, tn), lambda i, e: (0, 0, i)),
            pl.BlockSpec((te, d), lambda i, e: (e, 0)),
            pl.BlockSpec((d, te), lambda i, e: (0, e)),
            pl.BlockSpec((tn, d), lambda i, e: (i, 0), pipeline_mode=once),
            pl.BlockSpec((1, d), lambda i, e: (0, 0)),
        ],
        out_specs=pl.BlockSpec((tn, d), lambda i, e: (i, 0)),
        out_shape=jax.ShapeDtypeStruct((n, d), F32),
        scratch_shapes=[pltpu.VMEM((d, tn), F32),
                        pltpu.VMEM((PEER_HEADS, PEER_NKEYS, tn), F32),
                        pltpu.VMEM((te, tn), MXU_DTYPE),
                        pltpu.VMEM((2, te // PEER_NKEYS, PEER_HEADS, tn), F32),
                        pltpu.VMEM((te, tn), F32)],
        compiler_params=_cparams("parallel", "arbitrary"), name="peer_dense",
    )(xn, a, b, misc, u, vt, x, gain_final.reshape(1, d).astype(F32))


def _rope(x, pos):
    rot = x.shape[-1] // 4
    half = rot // 2
    inv = ROPE_THETA ** (-jnp.arange(half, dtype=F32) * 2.0 / rot)
    ang = pos.astype(F32)[:, None] * inv[None, :]
    cos = jnp.cos(ang)[None, :, None, :]
    sin = jnp.sin(ang)[None, :, None, :]
    x1 = x[..., :half]
    x2 = x[..., half:rot]
    return jnp.concatenate([x1 * cos - x2 * sin, x2 * cos + x1 * sin, x[..., rot:]], axis=-1)


def _l2norm(x):
    return x * lax.rsqrt(jnp.sum(x * x, axis=-1, keepdims=True) + EPS)


def _split_projection(p, pos, conv_prev, conv_w, a_log, dt_bias):
    b, t, _ = p.shape
    q = _rope(p[..., COL_Q:COL_K].reshape(b, t, ATTN_HEADS, HEAD_DIM), pos)
    k = _rope(p[..., COL_K:COL_V].reshape(b, t, ATTN_HEADS, HEAD_DIM), pos)
    v = p[..., COL_V:COL_QI].reshape(b, t, ATTN_HEADS, HEAD_DIM)
    qi = _rope(p[..., COL_QI:COL_SMALL].reshape(b, t, IDX_HEADS, IDX_DIM), pos)
    ki = _rope(p[..., COL_SMALL:COL_SMALL + IDX_DIM][:, :, None, :], pos)[:, :, 0, :]
    o = COL_SMALL + IDX_DIM
    wi = p[..., o:o + IDX_HEADS] * (IDX_HEADS * IDX_DIM) ** -0.5
    beta = jax.nn.sigmoid(p[..., o + IDX_HEADS:o + IDX_HEADS + DELTA_HEADS])
    a = p[..., o + IDX_HEADS + DELTA_HEADS:o + IDX_HEADS + 2 * DELTA_HEADS]
    g = -jnp.exp(a_log.astype(F32)) * jax.nn.softplus(a + dt_bias.astype(F32))
    dqkv = p[..., COL_DQKV:COL_Z]
    z = p[..., COL_Z:COL_END]
    xpad = jnp.concatenate([conv_prev.astype(F32), dqkv], axis=1)
    conv = sum(conv_w[i] * xpad[:, i:i + t] for i in range(CONV_WIDTH))
    conv = jax.nn.silu(conv)
    new_conv = xpad[:, t:]
    dq = _l2norm(conv[..., :DELTA_WIDTH].reshape(b, t, DELTA_HEADS, DELTA_DK)) * DELTA_DK ** -0.5
    dk = _l2norm(conv[..., DELTA_WIDTH:2 * DELTA_WIDTH].reshape(b, t, DELTA_HEADS, DELTA_DK))
    dv = conv[..., 2 * DELTA_WIDTH:]
    return (q, k, v, qi, ki, wi), (dq.reshape(b, t, DELTA_WIDTH), dk.reshape(b, t, DELTA_WIDTH), dv, g, beta, z), new_conv


def _pad_rows(x, tp):
    return jnp.pad(x, [(0, 0), (0, tp - x.shape[1])] + [(0, 0)] * (x.ndim - 2))


def _delta_group(dq, dk, dv, g, beta, z, s0, delta_norm):
    b, t, _ = dq.shape
    n_chunks = -(-t // DELTA_CHUNK)
    tp = n_chunks * DELTA_CHUNK

    def rows(x):
        return _pad_rows(x, tp).reshape(b, n_chunks, DELTA_CHUNK, DELTA_HEADS).transpose(0, 3, 1, 2)

    gcum = jnp.cumsum(rows(g), axis=-1)
    o, s = gated_delta(_pad_rows(dq, tp), _pad_rows(dk, tp), _pad_rows(dv, tp), _pad_rows(z, tp),
                       gcum, rows(beta), s0.astype(F32), delta_norm, n_chunks=n_chunks)
    return o[:, :t], s


def _sample_attention(q, k, v, qi, ki, wi, cache_k, cache_v, cache_kidx, page_table, topk):
    db, ds, h, hd = q.shape
    n_pool = cache_k.shape[0]

    def new_page(x):
        return _pad_rows(x.reshape(db, ds, -1), PAGE_SIZE)

    scores, scores_new = sample_scores(
        qi.reshape(db, ds * IDX_HEADS, IDX_DIM).astype(MXU_DTYPE), wi.reshape(db, ds * IDX_HEADS, 1),
        new_page(ki), cache_kidx, page_table, ds=ds)
    eye = jnp.eye(h, dtype=F32)
    q_exp = (q[:, :, :, None, :] * eye[None, None, :, :, None]).reshape(db, ds * h, h * hd).astype(MXU_DTYPE)
    o = sample_attend(scores, scores_new, q_exp, new_page(k), new_page(v),
                      cache_k.reshape(n_pool, PAGE_SIZE, h * hd), cache_v.reshape(n_pool, PAGE_SIZE, h * hd),
                      page_table, ds=ds, topk=topk)
    o = o.reshape(db, ds, h, h, hd)
    return jnp.stack([o[:, :, hh, hh] for hh in range(h)], axis=2).reshape(db, ds, h * hd)


def _token_tile(n, want):
    best = LANES
    for m in range(LANES, want + 1, LANES):
        if n % m == 0:
            best = m
    return best


def _layer(l, x, dims, cache_k, cache_v, cache_kidx, state_conv, state_ssm, page_table,
           norm_mix, w_in, conv_w, a_log, dt_bias, delta_norm, w_out, norm_ffn,
           peer_wq, peer_subkeys, peer_u, peer_v, norm_out):
    b, t, db, ds = dims
    n_pad, d = x.shape
    n_p, n_s = b * t, db * ds
    past = page_table.shape[1] * PAGE_SIZE
    tm = _token_tile(n_pad, 1280)

    w = w_in[l]
    o_wi = 3 * ATTN_WIDTH + IDX_HEADS * IDX_DIM + IDX_DIM + IDX_HEADS
    o_z = o_wi + 3 * DELTA_WIDTH
    o_b = o_z + DELTA_WIDTH
    w_perm = jnp.concatenate(
        [w[:, :o_wi], w[:, o_b:o_b + 2 * DELTA_HEADS],
         jnp.zeros((d, COL_DQKV - (o_wi + 2 * DELTA_HEADS)), w.dtype),
         w[:, o_wi:o_b], jnp.zeros((d, PROJ_PAD - COL_END), w.dtype)], axis=1).astype(MXU_DTYPE)
    p = norm_matmul(x, norm_mix[l], w_perm, tm=tm, tn=1024, out_dtype=F32, return_normed=False)

    pos_p = jnp.arange(t)
    (q, k_p, v_p, qi, ki_p, wi), (dq, dk, dv, g, beta, z), conv_p = _split_projection(
        p[:n_p].reshape(b, t, PROJ_PAD), pos_p, jnp.zeros((b, CONV_WIDTH - 1, 3 * DELTA_WIDTH), F32),
        conv_w[l], a_log[l], dt_bias[l])
    tp = _round_up(t, ATTN_QBLOCK)
    topk_p = min(TOPK_MAX, (t - N_META) // 4)
    attn_p = attn_prompt(
        _pad_rows(q.reshape(b, t, ATTN_WIDTH), tp).astype(MXU_DTYPE),
        _pad_rows(k_p.reshape(b, t, ATTN_WIDTH), tp).astype(MXU_DTYPE),
        _pad_rows(v_p.reshape(b, t, ATTN_WIDTH), tp).astype(MXU_DTYPE),
        _pad_rows(qi, tp).transpose(0, 2, 1, 3).astype(MXU_DTYPE),
        _pad_rows(ki_p, tp).astype(MXU_DTYPE), _pad_rows(wi, tp), tq=ATTN_QBLOCK, topk=topk_p)[:, :t]
    gated_p, ssm_p = _delta_group(dq, dk, dv, g, beta, z,
                                  jnp.zeros((b, DELTA_HEADS, DELTA_DK, DELTA_DV), F32), delta_norm[l])

    pos_s = past + jnp.arange(ds)
    (q, k_s, v_s, qi, ki_s, wi), (dq, dk, dv, g, beta, z), conv_s = _split_projection(
        p[n_p:n_p + n_s].reshape(db, ds, PROJ_PAD), pos_s, state_conv[l], conv_w[l], a_log[l], dt_bias[l])
    topk_s = min(TOPK_MAX, (past + ds) // 4)
    attn_s = _sample_attention(q, k_s, v_s, qi, ki_s, wi, cache_k[l], cache_v[l], cache_kidx[l],
                               page_table, topk_s).astype(MXU_DTYPE)
    gated_s, ssm_s = _delta_group(dq, dk, dv, g, beta, z, state_ssm[l], delta_norm[l])

    def flat(xp_, xs_):
        return jnp.concatenate([xp_.reshape(n_p, -1), xs_.reshape(n_s, -1),
                                jnp.zeros((n_pad - n_p - n_s, xp_.shape[-1]), xp_.dtype)], axis=0)

    wo = w_out[l].astype(MXU_DTYPE)
    x1 = matmul2_residual(flat(attn_p, attn_s), flat(gated_p, gated_s), wo[:ATTN_WIDTH], wo[ATTN_WIDTH:], x,
                          tm=tm, tn=1024 if d % 1024 == 0 else d)

    wq = peer_wq[l].astype(MXU_DTYPE)
    qp, xn = norm_matmul(x1, norm_ffn[l], wq, tm=tm, tn=wq.shape[1], out_dtype=MXU_DTYPE, return_normed=True)
    qp = qp.reshape(n_pad, 2 * PEER_HEADS, PEER_QDIM // 2).transpose(1, 0, 2)
    a, bb, misc = peer_select(qp, peer_subkeys[l].astype(MXU_DTYPE), tn=TOKEN_BLOCK)
    y = peer_dense(xn, a, bb, misc, peer_u[l].astype(MXU_DTYPE), peer_v[l].T.astype(MXU_DTYPE), x1, norm_out,
                   tn=TOKEN_BLOCK, te=PEER_EXPERT_BLOCK)
    caches_p = (k_p, v_p, ki_p, conv_p, ssm_p)
    caches_s = (k_s, v_s, ki_s, conv_s, ssm_s)
    return y, caches_p, caches_s


def kernel(x_prompt, x_sample, cache_k, cache_v, cache_kidx, state_conv, state_ssm, page_table, meta, norm_mix,
           w_in, conv_w, a_log, dt_bias, delta_norm, w_out, norm_ffn, peer_wq, peer_subkeys, peer_u, peer_v,
           norm_final):
    b, s, d = x_prompt.shape
    db, ds, _ = x_sample.shape
    t = s + N_META
    depth = w_in.shape[0]
    assert depth == 1, "the fused PEER + final-norm epilogue assumes a single layer"
    n_p, n_s = b * t, db * ds
    n_pad = _round_up(n_p + n_s, TOKEN_BLOCK)
    xp = jnp.concatenate([jnp.broadcast_to(meta[None].astype(F32), (b, N_META, d)), x_prompt], axis=1)
    x = jnp.concatenate([xp.reshape(n_p, d), x_sample.reshape(n_s, d), jnp.zeros((n_pad - n_p - n_s, d), F32)], axis=0)
    y, cp, cs = _layer(0, x, (b, t, db, ds), cache_k, cache_v, cache_kidx, state_conv, state_ssm, page_table,
                       norm_mix, w_in, conv_w, a_log, dt_bias, delta_norm, w_out, norm_ffn,
                       peer_wq, peer_subkeys, peer_u, peer_v, norm_final)
    y_prompt = y[:n_p].reshape(b, t, d)[:, N_META:]
    y_sample = y[n_p:n_p + n_s].reshape(db, ds, d)
    return (y_prompt, y_sample) + tuple(c[None] for c in cp) + tuple(c[None] for c in cs)
```

```python
import functools
import math

import jax
import jax.numpy as jnp
from jax import lax
from jax.experimental import pallas as pl
from jax.experimental.pallas import tpu as pltpu

F32 = jnp.float32
MXU_DTYPE = jnp.bfloat16

N_META = 16
HEAD_DIM = 128
ATTN_HEADS = 8
ATTN_WIDTH = ATTN_HEADS * HEAD_DIM
IDX_HEADS = 8
IDX_DIM = 64
TOPK_MAX = 256
ROPE_THETA = 500000.0
DELTA_DK = 128
DELTA_DV = 128
DELTA_HEADS = 8
DELTA_WIDTH = DELTA_HEADS * DELTA_DV
CONV_WIDTH = 4
PEER_HEADS = 8
PEER_NKEYS = 128
PEER_QDIM = 256
PEER_TOPK = 16
PAGE_SIZE = 128
EPS = 1e-6
NEG = -1e30

COL_Q = 0
COL_K = COL_Q + ATTN_WIDTH
COL_V = COL_K + ATTN_WIDTH
COL_QI = COL_V + ATTN_WIDTH
COL_SMALL = COL_QI + IDX_HEADS * IDX_DIM
COL_DQKV = COL_SMALL + 128
COL_Z = COL_DQKV + 3 * DELTA_WIDTH
COL_END = COL_Z + DELTA_WIDTH
PROJ_PAD = 8192

LANES = 128
TOKEN_BLOCK = 512
ATTN_QBLOCK = 256
DELTA_CHUNK = 128
DELTA_HEADS_PER_STEP = 4
SCORE_PAGES_PER_STEP = 8
ATTEND_PAGES_PER_STEP = 8
PEER_EXPERT_BLOCK = 1024
VMEM_LIMIT = 56 * 1024 * 1024

_NT = (((1,), (1,)), ((), ()))


def _round_up(x, m):
    return (x + m - 1) // m * m


def _cparams(*sem):
    return pltpu.CompilerParams(dimension_semantics=sem, vmem_limit_bytes=VMEM_LIMIT)


def _norm_matmul_kernel(x_ref, g_ref, w_ref, o_ref, xn_ref):
    @pl.when(pl.program_id(1) == 0)
    def _():
        x = x_ref[...]
        ms = jnp.mean(x * x, axis=-1, keepdims=True)
        xn_ref[...] = (x * lax.rsqrt(ms + EPS) * g_ref[...]).astype(xn_ref.dtype)

    o_ref[...] = jnp.dot(xn_ref[...], w_ref[...], preferred_element_type=F32).astype(o_ref.dtype)


def norm_matmul(x, gain, w, *, tm, tn, out_dtype, return_normed):
    n, d = x.shape
    width = w.shape[1]
    grid = (n // tm, width // tn)
    in_specs = [
        pl.BlockSpec((tm, d), lambda i, j: (i, 0)),
        pl.BlockSpec((1, d), lambda i, j: (0, 0)),
        pl.BlockSpec((d, tn), lambda i, j: (0, j)),
    ]
    o_spec = pl.BlockSpec((tm, tn), lambda i, j: (i, j))
    o_shape = jax.ShapeDtypeStruct((n, width), out_dtype)
    xn_spec = pl.BlockSpec((tm, d), lambda i, j: (i, 0))
    if return_normed:
        return pl.pallas_call(
            _norm_matmul_kernel, grid=grid, in_specs=in_specs,
            out_specs=[o_spec, xn_spec],
            out_shape=[o_shape, jax.ShapeDtypeStruct((n, d), MXU_DTYPE)],
            compiler_params=_cparams("parallel", "arbitrary"), name="norm_matmul_xn",
        )(x, gain.reshape(1, d).astype(F32), w)
    return pl.pallas_call(
        _norm_matmul_kernel, grid=grid, in_specs=in_specs, out_specs=o_spec, out_shape=o_shape,
        scratch_shapes=[pltpu.VMEM((tm, d), MXU_DTYPE)],
        compiler_params=_cparams("parallel", "arbitrary"), name="norm_matmul",
    )(x, gain.reshape(1, d).astype(F32), w)


def _matmul2_kernel(a1_ref, a2_ref, w1_ref, w2_ref, r_ref, o_ref):
    acc = jnp.dot(a1_ref[...], w1_ref[...], preferred_element_type=F32)
    acc = acc + jnp.dot(a2_ref[...], w2_ref[...], preferred_element_type=F32)
    o_ref[...] = acc + r_ref[...]


def matmul2_residual(a1, a2, w1, w2, r, *, tm, tn):
    n, k1 = a1.shape
    k2 = a2.shape[1]
    d = w1.shape[1]
    return pl.pallas_call(
        _matmul2_kernel, grid=(n // tm, d // tn),
        in_specs=[
            pl.BlockSpec((tm, k1), lambda i, j: (i, 0)),
            pl.BlockSpec((tm, k2), lambda i, j: (i, 0)),
            pl.BlockSpec((k1, tn), lambda i, j: (0, j)),
            pl.BlockSpec((k2, tn), lambda i, j: (0, j)),
            pl.BlockSpec((tm, tn), lambda i, j: (i, j)),
        ],
        out_specs=pl.BlockSpec((tm, tn), lambda i, j: (i, j)),
        out_shape=jax.ShapeDtypeStruct((n, d), F32),
        compiler_params=_cparams("parallel", "arbitrary"), name="out_proj",
    )(a1, a2, w1, w2, r)


def _order_key(s):
    s = jnp.where(s == 0.0, 0.0, s)
    key = lax.bitcast_convert_type(s, jnp.int32)
    return jnp.where(key < 0, key ^ jnp.int32(0x7FFFFFFF), key)


def _count_lanes(mask):
    return jnp.sum(jnp.where(mask, 1.0, 0.0), axis=-1, keepdims=True)


def _count_pages_lanes(mask):
    per_query = jnp.sum(jnp.where(mask, 1.0, 0.0), axis=0, keepdims=True)
    return jnp.sum(per_query, axis=2, keepdims=True)


def _topk_threshold(key, col, topk, idx_bits, count):
    kf = float(topk)
    nonneg = count(key >= 0) >= kf
    base = jnp.where(nonneg, jnp.int32(0), jnp.int32(-2 ** 31))

    def value_bit(it, base):
        cand = base | jnp.left_shift(jnp.int32(1), 30 - it)
        return jnp.where(count(key >= cand) >= kf, cand, base)

    thr = lax.fori_loop(0, 31, value_bit, base)
    tied = key == thr
    need = kf - count(key > thr)

    def index_bit(it, j):
        cand = j | jnp.left_shift(jnp.int32(1), idx_bits - 1 - it)
        return jnp.where(count(tied & (col < cand)) < need, cand, j)

    j = lax.fori_loop(0, idx_bits, index_bit, jnp.zeros_like(thr))
    return thr, j


def _attn_prompt_kernel(q_ref, k_ref, v_ref, qi_ref, ki_ref, wi_ref, o_ref, *, q_start, topk, idx_bits):
    tq = q_ref.shape[1]
    kl = k_ref.shape[1]
    ki = ki_ref[0]
    wi = wi_ref[0]
    s = jnp.zeros((tq, kl), F32)
    for h in range(IDX_HEADS):
        d = lax.dot_general(qi_ref[0, h], ki, _NT, preferred_element_type=F32)
        s = s + jnp.maximum(d, 0.0) * wi[:, h:h + 1]
    row = q_start + lax.broadcasted_iota(jnp.int32, (tq, kl), 0)
    col = lax.broadcasted_iota(jnp.int32, (tq, kl), 1)
    causal = col <= row
    key = _order_key(jnp.where(causal, s, NEG))
    thr, j = _topk_threshold(key, col, topk, idx_bits, _count_lanes)
    sel = (key > thr) | ((key == thr) & (col <= j))
    bias = jnp.where(sel & causal, 0.0, NEG)
    scale = HEAD_DIM ** -0.5
    for h in range(ATTN_HEADS):
        hs = slice(h * HEAD_DIM, (h + 1) * HEAD_DIM)
        logits = lax.dot_general(q_ref[0, :, hs], k_ref[0, :, hs], _NT, preferred_element_type=F32) * scale + bias
        m = jnp.max(logits, axis=-1, keepdims=True)
        p = jnp.exp(logits - m)
        l = jnp.sum(p, axis=-1, keepdims=True)
        o = jnp.dot(p.astype(v_ref.dtype), v_ref[0, :, hs], preferred_element_type=F32)
        o_ref[0, :, hs] = (o / l).astype(o_ref.dtype)


def attn_prompt(q, k, v, qi, ki, wi, *, tq, topk):
    b, tp, w = q.shape
    outs = []
    for i in range(tp // tq):
        kl = (i + 1) * tq
        kern = functools.partial(_attn_prompt_kernel, q_start=i * tq, topk=topk,
                                 idx_bits=max(1, (kl - 1).bit_length()))
        outs.append(pl.pallas_call(
            kern, grid=(b,),
            in_specs=[
                pl.BlockSpec((1, tq, w), lambda bb, i=i: (bb, i, 0)),
                pl.BlockSpec((1, kl, w), lambda bb: (bb, 0, 0)),
                pl.BlockSpec((1, kl, w), lambda bb: (bb, 0, 0)),
                pl.BlockSpec((1, IDX_HEADS, tq, IDX_DIM), lambda bb, i=i: (bb, 0, i, 0)),
                pl.BlockSpec((1, kl, IDX_DIM), lambda bb: (bb, 0, 0)),
                pl.BlockSpec((1, tq, IDX_HEADS), lambda bb, i=i: (bb, i, 0)),
            ],
            out_specs=pl.BlockSpec((1, tq, w), lambda bb: (bb, 0, 0)),
            out_shape=jax.ShapeDtypeStruct((b, tq, w), MXU_DTYPE),
            compiler_params=_cparams("parallel"), name=f"attn_prompt_q{i}",
        )(q, k, v, qi, ki, wi))
    return jnp.concatenate(outs, axis=1)


def _page_scores(qi, wcol, kpage, ds):
    d = lax.dot_general(qi, kpage.astype(qi.dtype), _NT, preferred_element_type=F32)
    r = jnp.maximum(d, 0.0) * wcol
    return jnp.sum(r.reshape(ds, IDX_HEADS, r.shape[-1]), axis=1)


def _sample_scores_kernel(pt_ref, qi_ref, wi_ref, knew_ref, *rest, n_pages_step, ds):
    page_refs = rest[:n_pages_step]
    past_ref, new_ref = rest[n_pages_step:]
    qi = qi_ref[0]
    wcol = wi_ref[0]
    for g in range(n_pages_step):
        past_ref[0, g] = _page_scores(qi, wcol, page_refs[g][0], ds)
    s_new = _page_scores(qi, wcol, knew_ref[0], ds)
    qrow = lax.broadcasted_iota(jnp.int32, s_new.shape, 0)
    kcol = lax.broadcasted_iota(jnp.int32, s_new.shape, 1)
    new_ref[0] = jnp.where(kcol <= qrow, s_new, NEG)


def sample_scores(qi, wi, ki_new_page, cache_kidx, page_table, *, ds):
    db, n_pages = page_table.shape
    g = math.gcd(SCORE_PAGES_PER_STEP, n_pages)
    rows = ds * IDX_HEADS
    page_specs = [pl.BlockSpec((1, PAGE_SIZE, IDX_DIM), lambda b, p, pt, gg=gg: (pt[b, p * g + gg], 0, 0))
                  for gg in range(g)]
    grid_spec = pltpu.PrefetchScalarGridSpec(
        num_scalar_prefetch=1, grid=(db, n_pages // g),
        in_specs=[pl.BlockSpec((1, rows, IDX_DIM), lambda b, p, pt: (b, 0, 0)),
                  pl.BlockSpec((1, rows, 1), lambda b, p, pt: (b, 0, 0)),
                  pl.BlockSpec((1, PAGE_SIZE, IDX_DIM), lambda b, p, pt: (b, 0, 0))] + page_specs,
        out_specs=[pl.BlockSpec((1, g, ds, PAGE_SIZE), lambda b, p, pt: (b, p, 0, 0)),
                   pl.BlockSpec((1, ds, PAGE_SIZE), lambda b, p, pt: (b, 0, 0))])
    return pl.pallas_call(
        functools.partial(_sample_scores_kernel, n_pages_step=g, ds=ds), grid_spec=grid_spec,
        out_shape=[jax.ShapeDtypeStruct((db, n_pages, ds, PAGE_SIZE), F32),
                   jax.ShapeDtypeStruct((db, ds, PAGE_SIZE), F32)],
        compiler_params=_cparams("parallel", "arbitrary"), name="sample_scores",
    )(page_table, qi, wi, ki_new_page, *([cache_kidx] * g))


def _sample_attend_kernel(pt_ref, sc_ref, scn_ref, q_ref, kn_ref, vn_ref, *rest, n_pages_step, ds, topk, idx_bits):
    k_refs = rest[:n_pages_step]
    v_refs = rest[n_pages_step:2 * n_pages_step]
    o_ref, thr_ref, j_ref, m_ref, l_ref, acc_ref = rest[2 * n_pages_step:]
    p = pl.program_id(1)
    n_pages = sc_ref.shape[1]
    nh = ATTN_HEADS
    scale = HEAD_DIM ** -0.5
    sub = lax.broadcasted_iota(jnp.int32, (nh, LANES), 0)
    lane = lax.broadcasted_iota(jnp.int32, (nh, LANES), 1)
    head_bias = jnp.where((lane < ds * nh) & (lane % nh == sub), 0.0, NEG)
    qrow = lax.broadcasted_iota(jnp.int32, (LANES, LANES), 0)
    qlane = lax.broadcasted_iota(jnp.int32, (LANES, LANES), 1)
    expand = jnp.where((qrow < ds) & (qlane // nh == qrow), 1.0, 0.0).astype(q_ref.dtype)

    @pl.when(p == 0)
    def _():
        key = _order_key(jnp.concatenate([sc_ref[0], scn_ref[0][None]], axis=0))
        col = (lax.broadcasted_iota(jnp.int32, key.shape, 0) * PAGE_SIZE
               + lax.broadcasted_iota(jnp.int32, key.shape, 2))
        thr, j = _topk_threshold(key, col, topk, idx_bits, _count_pages_lanes)
        thr_ref[...] = thr[0]
        j_ref[...] = j[0]
        m_ref[...] = jnp.full_like(m_ref, NEG)
        l_ref[...] = jnp.zeros_like(l_ref)
        acc_ref[...] = jnp.zeros_like(acc_ref)

    def attend(scores, page_idx, kpage, vpage):
        key = _order_key(scores)
        col = page_idx * PAGE_SIZE + lax.broadcasted_iota(jnp.int32, key.shape, 1)
        sel = ((key > thr_ref[...]) | ((key == thr_ref[...]) & (col <= j_ref[...]))) & (scores > 0.5 * NEG)
        sel_rows = jnp.concatenate([jnp.where(sel, 1.0, 0.0), jnp.zeros((LANES - ds, PAGE_SIZE), F32)], axis=0)
        spread = jnp.dot(sel_rows.T.astype(q_ref.dtype), expand, preferred_element_type=F32)
        key_bias = jnp.where(spread > 0.5, 0.0, NEG)
        bias = (key_bias[:, None, :] + head_bias[None, :, :]).reshape(PAGE_SIZE * nh, LANES)
        logits = jnp.dot(kpage.astype(q_ref.dtype), q_ref[0], preferred_element_type=F32) * scale + bias
        m_new = jnp.maximum(m_ref[...], jnp.max(logits, axis=0, keepdims=True))
        alpha = jnp.exp(m_ref[...] - m_new)
        pr = jnp.exp(logits - m_new)
        l_ref[...] = alpha * l_ref[...] + jnp.sum(pr, axis=0, keepdims=True)
        acc_ref[...] = alpha * acc_ref[...] + jnp.dot(vpage.T.astype(q_ref.dtype), pr.astype(q_ref.dtype),
                                                      preferred_element_type=F32)
        m_ref[...] = m_new

    for g in range(n_pages_step):
        page_idx = p * n_pages_step + g
        attend(sc_ref[0, page_idx], page_idx, k_refs[g][0], v_refs[g][0])

    @pl.when(p == pl.num_programs(1) - 1)
    def _():
        attend(scn_ref[0], n_pages, kn_ref[0], vn_ref[0])
        o_ref[0] = acc_ref[...] / jnp.where(l_ref[...] > 0.0, l_ref[...], 1.0)


def sample_attend(scores, scores_new, q_cols, k_new_page, v_new_page, cache_k, cache_v, page_table, *, ds, topk):
    db, n_pages = page_table.shape
    g = math.gcd(ATTEND_PAGES_PER_STEP, n_pages)
    page_rows = PAGE_SIZE * ATTN_HEADS
    idx_bits = max(1, ((n_pages + 1) * PAGE_SIZE - 1).bit_length())

    def page_spec(gg):
        return pl.BlockSpec((1, page_rows, HEAD_DIM), lambda b, p, pt, gg=gg: (pt[b, p * g + gg], 0, 0))

    grid_spec = pltpu.PrefetchScalarGridSpec(
        num_scalar_prefetch=1, grid=(db, n_pages // g),
        in_specs=[pl.BlockSpec((1, n_pages, ds, PAGE_SIZE), lambda b, p, pt: (b, 0, 0, 0)),
                  pl.BlockSpec((1, ds, PAGE_SIZE), lambda b, p, pt: (b, 0, 0)),
                  pl.BlockSpec((1, HEAD_DIM, LANES), lambda b, p, pt: (b, 0, 0)),
                  pl.BlockSpec((1, page_rows, HEAD_DIM), lambda b, p, pt: (b, 0, 0)),
                  pl.BlockSpec((1, page_rows, HEAD_DIM), lambda b, p, pt: (b, 0, 0))]
        + [page_spec(gg) for gg in range(g)] * 2,
        out_specs=pl.BlockSpec((1, HEAD_DIM, LANES), lambda b, p, pt: (b, 0, 0)),
        scratch_shapes=[pltpu.VMEM((ds, 1), jnp.int32), pltpu.VMEM((ds, 1), jnp.int32),
                        pltpu.VMEM((1, LANES), F32), pltpu.VMEM((1, LANES), F32), pltpu.VMEM((HEAD_DIM, LANES), F32)])
    kern = functools.partial(_sample_attend_kernel, n_pages_step=g, ds=ds, topk=topk, idx_bits=idx_bits)
    return pl.pallas_call(
        kern, grid_spec=grid_spec, out_shape=jax.ShapeDtypeStruct((db, HEAD_DIM, LANES), F32),
        compiler_params=_cparams("parallel", "arbitrary"), name="sample_attend",
    )(page_table, scores, scores_new, q_cols, k_new_page, v_new_page, *([cache_k] * g), *([cache_v] * g))


def _split_hi_lo(x):
    hi = x.astype(MXU_DTYPE)
    return hi, (x - hi.astype(F32)).astype(MXU_DTYPE)


def _dot1(a, b, dims=None):
    if dims is None:
        dims = (((1,), (0,)), ((), ()))
    return lax.dot_general(a.astype(MXU_DTYPE), b.astype(MXU_DTYPE), dims, preferred_element_type=F32)


def _dot3(a, b, dims=None):
    ah, al = _split_hi_lo(a)
    bh, bl = _split_hi_lo(b)
    if dims is None:
        dims = (((1,), (0,)), ((), ()))
    d = functools.partial(lax.dot_general, dimension_numbers=dims, preferred_element_type=F32)
    return d(ah, bh) + (d(ah, bl) + d(al, bh))


def _delta_kernel(q_ref, k_ref, v_ref, z_ref, gcum_ref, beta_ref, s0_ref, dn_ref, o_ref, s_ref, t_scr, qk_scr,
                  *, n_chunks):
    c_len = DELTA_CHUNK
    n_h = s0_ref.shape[1]
    ri = lax.broadcasted_iota(jnp.int32, (c_len, c_len), 0)
    ci = lax.broadcasted_iota(jnp.int32, (c_len, c_len), 1)
    incl = ri >= ci
    strict = ri > ci
    eye = jnp.where(ri == ci, 1.0, 0.0)
    dn = dn_ref[...]

    def chunk_rows(c):
        return pl.ds(pl.multiple_of(c * c_len, c_len), c_len)

    def decay_terms(c, hh):
        g_row = jnp.broadcast_to(gcum_ref[0, hh, pl.ds(c, 1), :], (c_len, c_len))
        b_col = jnp.broadcast_to(beta_ref[0, hh, pl.ds(c, 1), :], (c_len, c_len)).T
        return g_row, g_row.T, b_col

    heads = range(n_h)
    head_cols = [slice(hh * DELTA_DK, (hh + 1) * DELTA_DK) for hh in heads]

    def build(c, carry):
        rows = chunk_rows(c)
        qc = [q_ref[0, rows, cs] for cs in head_cols]
        kc = [k_ref[0, rows, cs] for cs in head_cols]
        terms = [decay_terms(c, hh) for hh in heads]
        dec = [jnp.where(incl, jnp.exp(jnp.where(incl, g_col - g_row, 0.0)), 0.0) for g_row, g_col, _ in terms]
        kk = [_dot3(kc[hh], kc[hh], _NT) for hh in heads]
        a = [jnp.where(strict, kk[hh] * dec[hh], 0.0) * terms[hh][2] for hh in heads]
        t = [eye - a[hh] for hh in heads]
        pw = [_dot1(a[hh], a[hh]) for hh in heads]
        span = 2
        while span < c_len:
            t = [t[hh] + _dot1(t[hh], pw[hh]) for hh in heads]
            span *= 2
            if span < c_len:
                pw = [_dot1(pw[hh], pw[hh]) for hh in heads]
        at = [_dot3(a[hh], t[hh]) for hh in heads]
        corr = [_dot3(t[hh], eye - t[hh] - at[hh]) for hh in heads]
        qk = [_dot1(qc[hh], kc[hh], _NT) for hh in heads]
        for hh in heads:
            t_scr[hh, c] = t[hh] + corr[hh]
            qk_scr[hh, c] = qk[hh] * dec[hh]
        return carry

    lax.fori_loop(0, n_chunks, build, 0)

    def step(c, states):
        rows = chunk_rows(c)
        qc = [q_ref[0, rows, cs] for cs in head_cols]
        kc = [k_ref[0, rows, cs] for cs in head_cols]
        terms = [decay_terms(c, hh) for hh in heads]
        e_g = [jnp.exp(g_col) for _, g_col, _ in terms]
        ks = [_dot1(kc[hh], states[hh]) for hh in heads]
        qs = [_dot1(qc[hh], states[hh]) for hh in heads]
        u = [_dot3(t_scr[hh, c], terms[hh][2] * (v_ref[0, rows, head_cols[hh]] - e_g[hh] * ks[hh])) for hh in heads]
        qku = [_dot1(qk_scr[hh, c], u[hh]) for hh in heads]
        g_last = [g_row[:, c_len - 1:c_len] for g_row, _, _ in terms]
        kw = [kc[hh] * jnp.exp(g_last[hh] - terms[hh][1]) for hh in heads]
        upd = [_dot1(kw[hh].T, u[hh]) for hh in heads]
        for hh in heads:
            o = e_g[hh] * qs[hh] + qku[hh]
            z = z_ref[0, rows, head_cols[hh]]
            on = o * lax.rsqrt(jnp.mean(o * o, axis=-1, keepdims=True) + EPS) * dn
            o_ref[0, rows, head_cols[hh]] = (on * (z / (1.0 + jnp.exp(-z)))).astype(o_ref.dtype)
        return tuple(jnp.exp(g_last[hh]) * states[hh] + upd[hh] for hh in heads)

    final = lax.fori_loop(0, n_chunks, step, tuple(s0_ref[0, hh] for hh in range(n_h)))
    for hh in range(n_h):
        s_ref[0, hh] = final[hh]


def gated_delta(q, k, v, z, gcum, beta, s0, delta_norm, *, n_chunks):
    b, tp, _ = q.shape
    h = DELTA_HEADS
    n_h = DELTA_HEADS_PER_STEP
    tok = pl.BlockSpec((1, tp, n_h * DELTA_DV), lambda bb, hh: (bb, 0, hh))
    row = pl.BlockSpec((1, n_h, n_chunks, DELTA_CHUNK), lambda bb, hh: (bb, hh, 0, 0))
    st = pl.BlockSpec((1, n_h, DELTA_DK, DELTA_DV), lambda bb, hh: (bb, hh, 0, 0))
    mat = pltpu.VMEM((n_h, n_chunks, DELTA_CHUNK, DELTA_CHUNK), F32)
    return pl.pallas_call(
        functools.partial(_delta_kernel, n_chunks=n_chunks), grid=(b, h // n_h),
        in_specs=[tok, tok, tok, tok, row, row, st, pl.BlockSpec((1, DELTA_DV), lambda bb, hh: (0, 0))],
        out_specs=[tok, st],
        out_shape=[jax.ShapeDtypeStruct((b, tp, h * DELTA_DV), MXU_DTYPE),
                   jax.ShapeDtypeStruct((b, h, DELTA_DK, DELTA_DV), F32)],
        scratch_shapes=[mat, mat],
        compiler_params=_cparams("parallel", "parallel"), name="gated_delta",
    )(q, k, v, z, gcum, beta, s0, delta_norm.reshape(1, DELTA_DV).astype(F32))


def _top_values(cur, n):
    vals = []
    for r in range(n):
        m = jnp.max(cur, axis=0, keepdims=True)
        vals.append(m)
        if r + 1 < n:
            cur = jnp.where(cur == m, -jnp.inf, cur)
    return vals


_PEER_RANKS = PEER_TOPK + 1
_PEER_PAIRS = [(r, c) for r in range(_PEER_RANKS) for c in range(_PEER_RANKS) if (r + 1) * (c + 1) <= _PEER_RANKS]


def _peer_select_kernel(q_ref, sk_ref, a_ref, b_ref, misc_ref):
    tn = q_ref.shape[1]

    def head(h, carry):
        shifted = []
        tops = []
        for c in range(2):
            st = lax.dot_general(sk_ref[c], q_ref[h * 2 + c], _NT, preferred_element_type=F32)
            vals = _top_values(st, _PEER_RANKS)
            shifted.append(st - vals[0])
            tops.append([v - vals[0] for v in vals])
        a_ref[h] = shifted[0]
        b_ref[h] = shifted[1]
        cand = [tops[0][r] + tops[1][c] for r, c in _PEER_PAIRS]
        pad = _round_up(len(cand), 8) - len(cand)
        cand = jnp.concatenate(cand + [jnp.full((pad, tn), -jnp.inf, F32)], axis=0)
        best = _top_values(cand, _PEER_RANKS)
        zsum = jnp.zeros_like(best[0])
        for v in best[:PEER_TOPK]:
            zsum = zsum + jnp.exp(v)
        thr = 0.5 * (best[PEER_TOPK - 1] + best[PEER_TOPK])
        misc_ref[h] = jnp.concatenate([thr, 1.0 / zsum, jnp.zeros((6, tn), F32)], axis=0)
        return carry

    lax.fori_loop(0, PEER_HEADS, head, 0)


def peer_select(q, subkeys, *, tn):
    n = q.shape[1]
    blk = pl.BlockSpec((PEER_HEADS, PEER_NKEYS, tn), lambda i: (0, 0, i))
    return pl.pallas_call(
        _peer_select_kernel, grid=(n // tn,),
        in_specs=[pl.BlockSpec((q.shape[0], tn, q.shape[2]), lambda i: (0, i, 0)),
                  pl.BlockSpec(subkeys.shape, lambda i: (0, 0, 0))],
        out_specs=[blk, blk, pl.BlockSpec((PEER_HEADS, 8, tn), lambda i: (0, 0, i))],
        out_shape=[jax.ShapeDtypeStruct((PEER_HEADS, PEER_NKEYS, n), F32),
                   jax.ShapeDtypeStruct((PEER_HEADS, PEER_NKEYS, n), F32),
                   jax.ShapeDtypeStruct((PEER_HEADS, 8, n), F32)],
        compiler_params=_cparams("parallel"), name="peer_select",
    )(q, subkeys)


def _gelu_tanh(x):
    return 0.5 * x * (1.0 + jnp.tanh(math.sqrt(2.0 / math.pi) * (x + 0.044715 * (x * x * x))))


def _peer_dense_kernel(xn_ref, a_ref, b_ref, misc_ref, u_ref, vt_ref, x_ref, gf_ref, o_ref,
                       acc_ref, bexp_ref, g_ref, row_ref, ht_ref):
    e = pl.program_id(1)
    te = u_ref.shape[0]
    tn = xn_ref.shape[0]
    nk = PEER_NKEYS

    @pl.when(e == 0)
    def _():
        acc_ref[...] = jnp.zeros_like(acc_ref)
        for h in range(PEER_HEADS):
            bexp_ref[h] = jnp.exp(b_ref[h])

    n_i = te // nk
    col_w = min(tn, 2 * LANES)
    n_col = tn // col_w

    def up_piece(c):
        cs = slice(c * col_w, (c + 1) * col_w)
        ht_ref[:, cs] = lax.dot_general(u_ref[...], xn_ref[cs, :], _NT, preferred_element_type=F32)

    def down_piece(c):
        cs = slice(c * col_w, (c + 1) * col_w)
        acc_ref[:, cs] += jnp.dot(vt_ref[...], g_ref[:, cs], preferred_element_type=F32)

    thr_all = jnp.concatenate([misc_ref[h, 0:1, :] for h in range(PEER_HEADS)], axis=0)
    invz_all = jnp.concatenate([misc_ref[h, 1:2, :] for h in range(PEER_HEADS)], axis=0)
    for ii in range(n_i):
        a_rows = jnp.concatenate([a_ref[h, pl.ds(e * n_i + ii, 1), :] for h in range(PEER_HEADS)], axis=0)
        row_ref[0, ii] = thr_all - a_rows
        row_ref[1, ii] = jnp.exp(a_rows) * invz_all

    key_tile = 16

    def gate_columns(c):
        for tb in range(c * col_w // LANES, (c + 1) * col_w // LANES):
            lanes = slice(tb * LANES, (tb + 1) * LANES)
            for kt in range(nk // key_tile):
                keys = slice(kt * key_tile, (kt + 1) * key_tile)
                w = [None] * n_i
                for h in range(PEER_HEADS):
                    b_tile = b_ref[h, keys, lanes]
                    bexp_tile = bexp_ref[h, keys, lanes]
                    for ii in range(n_i):
                        wh = (jnp.where(b_tile >= row_ref[0, ii, h:h + 1, lanes], bexp_tile, 0.0)
                              * row_ref[1, ii, h:h + 1, lanes])
                        w[ii] = wh if w[ii] is None else w[ii] + wh
                for ii in range(n_i):
                    rows = slice(ii * nk + kt * key_tile, ii * nk + (kt + 1) * key_tile)
                    g_ref[rows, lanes] = (w[ii] * _gelu_tanh(ht_ref[rows, lanes])).astype(g_ref.dtype)

    up_piece(0)
    for c in range(n_col):
        if c + 1 < n_col:
            up_piece(c + 1)
        gate_columns(c)
        down_piece(c)

    @pl.when(e == pl.num_programs(1) - 1)
    def _():
        y = acc_ref[...].T + x_ref[...]
        ms = jnp.mean(y * y, axis=-1, keepdims=True)
        o_ref[...] = y * lax.rsqrt(ms + EPS) * gf_ref[...]


def peer_dense(xn, a, b, misc, u, vt, x, gain_final, *, tn, te):
    n, d = x.shape
    n_blocks = u.shape[0] // te
    once = pl.Buffered(1)
    sel = pl.BlockSpec((PEER_HEADS, PEER_NKEYS, tn), lambda i, e: (0, 0, i), pipeline_mode=once)
    return pl.pallas_call(
        _peer_dense_kernel, grid=(n // tn, n_blocks),
        in_specs=[
            pl.BlockSpec((tn, d), lambda i, e: (i, 0), pipeline_mode=once),
            sel, sel,
            pl.BlockSpec((PEER_HEADS, 8, tn), lambda i, e: (0, 0, i)),
            pl.BlockSpec((te, d), lambda i, e: (e, 0)),
            pl.BlockSpec((d, te), lambda i, e: (0, e)),
            pl.BlockSpec((tn, d), lambda i, e: (i, 0), pipeline_mode=once),
            pl.BlockSpec((1, d), lambda i, e: (0, 0)),
        ],
        out_specs=pl.BlockSpec((tn, d), lambda i, e: (i, 0)),
        out_shape=jax.ShapeDtypeStruct((n, d), F32),
        scratch_shapes=[pltpu.VMEM((d, tn), F32),
                        pltpu.VMEM((PEER_HEADS, PEER_NKEYS, tn), F32),
                        pltpu.VMEM((te, tn), MXU_DTYPE),
                        pltpu.VMEM((2, te // PEER_NKEYS, PEER_HEADS, tn), F32),
                        pltpu.VMEM((te, tn), F32)],
        compiler_params=_cparams("parallel", "arbitrary"), name="peer_dense",
    )(xn, a, b, misc, u, vt, x, gain_final.reshape(1, d).astype(F32))


def _rope(x, pos):
    rot = x.shape[-1] // 4
    half = rot // 2
    inv = ROPE_THETA ** (-jnp.arange(half, dtype=F32) * 2.0 / rot)
    ang = pos.astype(F32)[:, None] * inv[None, :]
    cos = jnp.cos(ang)[None, :, None, :]
    sin = jnp.sin(ang)[None, :, None, :]
    x1 = x[..., :half]
    x2 = x[..., half:rot]
    return jnp.concatenate([x1 * cos - x2 * sin, x2 * cos + x1 * sin, x[..., rot:]], axis=-1)


def _l2norm(x):
    return x * lax.rsqrt(jnp.sum(x * x, axis=-1, keepdims=True) + EPS)


def _split_projection(p, pos, conv_prev, conv_w, a_log, dt_bias):
    b, t, _ = p.shape
    q = _rope(p[..., COL_Q:COL_K].reshape(b, t, ATTN_HEADS, HEAD_DIM), pos)
    k = _rope(p[..., COL_K:COL_V].reshape(b, t, ATTN_HEADS, HEAD_DIM), pos)
    v = p[..., COL_V:COL_QI].reshape(b, t, ATTN_HEADS, HEAD_DIM)
    qi = _rope(p[..., COL_QI:COL_SMALL].reshape(b, t, IDX_HEADS, IDX_DIM), pos)
    ki = _rope(p[..., COL_SMALL:COL_SMALL + IDX_DIM][:, :, None, :], pos)[:, :, 0, :]
    o = COL_SMALL + IDX_DIM
    wi = p[..., o:o + IDX_HEADS] * (IDX_HEADS * IDX_DIM) ** -0.5
    beta = jax.nn.sigmoid(p[..., o + IDX_HEADS:o + IDX_HEADS + DELTA_HEADS])
    a = p[..., o + IDX_HEADS + DELTA_HEADS:o + IDX_HEADS + 2 * DELTA_HEADS]
    g = -jnp.exp(a_log.astype(F32)) * jax.nn.softplus(a + dt_bias.astype(F32))
    dqkv = p[..., COL_DQKV:COL_Z]
    z = p[..., COL_Z:COL_END]
    xpad = jnp.concatenate([conv_prev.astype(F32), dqkv], axis=1)
    conv = sum(conv_w[i] * xpad[:, i:i + t] for i in range(CONV_WIDTH))
    conv = jax.nn.silu(conv)
    new_conv = xpad[:, t:]
    dq = _l2norm(conv[..., :DELTA_WIDTH].reshape(b, t, DELTA_HEADS, DELTA_DK)) * DELTA_DK ** -0.5
    dk = _l2norm(conv[..., DELTA_WIDTH:2 * DELTA_WIDTH].reshape(b, t, DELTA_HEADS, DELTA_DK))
    dv = conv[..., 2 * DELTA_WIDTH:]
    return (q, k, v, qi, ki, wi), (dq.reshape(b, t, DELTA_WIDTH), dk.reshape(b, t, DELTA_WIDTH), dv, g, beta, z), new_conv


def _pad_rows(x, tp):
    return jnp.pad(x, [(0, 0), (0, tp - x.shape[1])] + [(0, 0)] * (x.ndim - 2))


def _delta_group(dq, dk, dv, g, beta, z, s0, delta_norm):
    b, t, _ = dq.shape
    n_chunks = -(-t // DELTA_CHUNK)
    tp = n_chunks * DELTA_CHUNK

    def rows(x):
        return _pad_rows(x, tp).reshape(b, n_chunks, DELTA_CHUNK, DELTA_HEADS).transpose(0, 3, 1, 2)

    gcum = jnp.cumsum(rows(g), axis=-1)
    o, s = gated_delta(_pad_rows(dq, tp), _pad_rows(dk, tp), _pad_rows(dv, tp), _pad_rows(z, tp),
                       gcum, rows(beta), s0.astype(F32), delta_norm, n_chunks=n_chunks)
    return o[:, :t], s


def _sample_attention(q, k, v, qi, ki, wi, cache_k, cache_v, cache_kidx, page_table, topk):
    db, ds, h, hd = q.shape
    n_pool = cache_k.shape[0]

    def new_page(x):
        return _pad_rows(x.reshape(db, ds, -1), PAGE_SIZE)

    scores, scores_new = sample_scores(
        qi.reshape(db, ds * IDX_HEADS, IDX_DIM).astype(MXU_DTYPE), wi.reshape(db, ds * IDX_HEADS, 1),
        new_page(ki), cache_kidx, page_table, ds=ds)
    q_cols = jnp.pad(q.reshape(db, ds * h, hd).transpose(0, 2, 1), ((0, 0), (0, 0), (0, LANES - ds * h)))

    def key_head_rows(x):
        return x.reshape(x.shape[:-3] + (x.shape[-3] * h, hd))

    o = sample_attend(scores, scores_new, q_cols.astype(MXU_DTYPE),
                      key_head_rows(_pad_rows(k, PAGE_SIZE)), key_head_rows(_pad_rows(v, PAGE_SIZE)),
                      key_head_rows(cache_k), key_head_rows(cache_v), page_table, ds=ds, topk=topk)
    return o[:, :, :ds * h].transpose(0, 2, 1).reshape(db, ds, h * hd)


def _token_tile(n, want):
    best = LANES
    for m in range(LANES, want + 1, LANES):
        if n % m == 0:
            best = m
    return best


def _layer(l, x, dims, cache_k, cache_v, cache_kidx, state_conv, state_ssm, page_table,
           norm_mix, w_in, conv_w, a_log, dt_bias, delta_norm, w_out, norm_ffn,
           peer_wq, peer_subkeys, peer_u, peer_v, norm_out):
    b, t, db, ds = dims
    n_pad, d = x.shape
    n_p, n_s = b * t, db * ds
    past = page_table.shape[1] * PAGE_SIZE
    tm = _token_tile(n_pad, 1280)

    w = w_in[l]
    o_wi = 3 * ATTN_WIDTH + IDX_HEADS * IDX_DIM + IDX_DIM + IDX_HEADS
    o_z = o_wi + 3 * DELTA_WIDTH
    o_b = o_z + DELTA_WIDTH
    w_perm = jnp.concatenate(
        [w[:, :o_wi], w[:, o_b:o_b + 2 * DELTA_HEADS],
         jnp.zeros((d, COL_DQKV - (o_wi + 2 * DELTA_HEADS)), w.dtype),
         w[:, o_wi:o_b], jnp.zeros((d, PROJ_PAD - COL_END), w.dtype)], axis=1).astype(MXU_DTYPE)
    p = norm_matmul(x, norm_mix[l], w_perm, tm=tm, tn=1024, out_dtype=F32, return_normed=False)

    pos_p = jnp.arange(t)
    (q, k_p, v_p, qi, ki_p, wi), (dq, dk, dv, g, beta, z), conv_p = _split_projection(
        p[:n_p].reshape(b, t, PROJ_PAD), pos_p, jnp.zeros((b, CONV_WIDTH - 1, 3 * DELTA_WIDTH), F32),
        conv_w[l], a_log[l], dt_bias[l])
    tp = _round_up(t, ATTN_QBLOCK)
    topk_p = min(TOPK_MAX, (t - N_META) // 4)
    attn_p = attn_prompt(
        _pad_rows(q.reshape(b, t, ATTN_WIDTH), tp).astype(MXU_DTYPE),
        _pad_rows(k_p.reshape(b, t, ATTN_WIDTH), tp).astype(MXU_DTYPE),
        _pad_rows(v_p.reshape(b, t, ATTN_WIDTH), tp).astype(MXU_DTYPE),
        _pad_rows(qi, tp).transpose(0, 2, 1, 3).astype(MXU_DTYPE),
        _pad_rows(ki_p, tp).astype(MXU_DTYPE), _pad_rows(wi, tp), tq=ATTN_QBLOCK, topk=topk_p)[:, :t]
    gated_p, ssm_p = _delta_group(dq, dk, dv, g, beta, z,
                                  jnp.zeros((b, DELTA_HEADS, DELTA_DK, DELTA_DV), F32), delta_norm[l])

    pos_s = past + jnp.arange(ds)
    (q, k_s, v_s, qi, ki_s, wi), (dq, dk, dv, g, beta, z), conv_s = _split_projection(
        p[n_p:n_p + n_s].reshape(db, ds, PROJ_PAD), pos_s, state_conv[l], conv_w[l], a_log[l], dt_bias[l])
    topk_s = min(TOPK_MAX, (past + ds) // 4)
    attn_s = _sample_attention(q, k_s, v_s, qi, ki_s, wi, cache_k[l], cache_v[l], cache_kidx[l],
                               page_table, topk_s).astype(MXU_DTYPE)
    gated_s, ssm_s = _delta_group(dq, dk, dv, g, beta, z, state_ssm[l], delta_norm[l])

    def flat(xp_, xs_):
        return jnp.concatenate([xp_.reshape(n_p, -1), xs_.reshape(n_s, -1),
                                jnp.zeros((n_pad - n_p - n_s, xp_.shape[-1]), xp_.dtype)], axis=0)

    wo = w_out[l].astype(MXU_DTYPE)
    x1 = matmul2_residual(flat(attn_p, attn_s), flat(gated_p, gated_s), wo[:ATTN_WIDTH], wo[ATTN_WIDTH:], x,
                          tm=tm, tn=1024 if d % 1024 == 0 else d)

    wq = peer_wq[l].astype(MXU_DTYPE)
    qp, xn = norm_matmul(x1, norm_ffn[l], wq, tm=tm, tn=wq.shape[1], out_dtype=MXU_DTYPE, return_normed=True)
    qp = qp.reshape(n_pad, 2 * PEER_HEADS, PEER_QDIM // 2).transpose(1, 0, 2)
    a, bb, misc = peer_select(qp, peer_subkeys[l].astype(MXU_DTYPE), tn=TOKEN_BLOCK)
    y = peer_dense(xn, a, bb, misc, peer_u[l].astype(MXU_DTYPE), peer_v[l].T.astype(MXU_DTYPE), x1, norm_out,
                   tn=TOKEN_BLOCK, te=PEER_EXPERT_BLOCK)
    caches_p = (k_p, v_p, ki_p, conv_p, ssm_p)
    caches_s = (k_s, v_s, ki_s, conv_s, ssm_s)
    return y, caches_p, caches_s


def kernel(x_prompt, x_sample, cache_k, cache_v, cache_kidx, state_conv, state_ssm, page_table, meta, norm_mix,
           w_in, conv_w, a_log, dt_bias, delta_norm, w_out, norm_ffn, peer_wq, peer_subkeys, peer_u, peer_v,
           norm_final):
    b, s, d = x_prompt.shape
    db, ds, _ = x_sample.shape
    t = s + N_META
    depth = w_in.shape[0]
    assert depth == 1, "the fused PEER + final-norm epilogue assumes a single layer"
    n_p, n_s = b * t, db * ds
    n_pad = _round_up(n_p + n_s, TOKEN_BLOCK)
    xp = jnp.concatenate([jnp.broadcast_to(meta[None].astype(F32), (b, N_META, d)), x_prompt], axis=1)
    x = jnp.concatenate([xp.reshape(n_p, d), x_sample.reshape(n_s, d), jnp.zeros((n_pad - n_p - n_s, d), F32)], axis=0)
    y, cp, cs = _layer(0, x, (b, t, db, ds), cache_k, cache_v, cache_kidx, state_conv, state_ssm, page_table,
                       norm_mix, w_in, conv_w, a_log, dt_bias, delta_norm, w_out, norm_ffn,
                       peer_wq, peer_subkeys, peer_u, peer_v, norm_final)
    y_prompt = y[:n_p].reshape(b, t, d)[:, N_META:]
    y_sample = y[n_p:n_p + n_s].reshape(db, ds, d)
    return (y_prompt, y_sample) + tuple(c[None] for c in cp) + tuple(c[None] for c in cs)
```

```python
import functools
import math

import jax
import jax.numpy as jnp
from jax import lax
from jax.experimental import pallas as pl
from jax.experimental.pallas import tpu as pltpu

F32 = jnp.float32
MXU_DTYPE = jnp.bfloat16

N_META = 16
HEAD_DIM = 128
ATTN_HEADS = 8
ATTN_WIDTH = ATTN_HEADS * HEAD_DIM
IDX_HEADS = 8
IDX_DIM = 64
TOPK_MAX = 256
ROPE_THETA = 500000.0
DELTA_DK = 128
DELTA_DV = 128
DELTA_HEADS = 8
DELTA_WIDTH = DELTA_HEADS * DELTA_DV
CONV_WIDTH = 4
PEER_HEADS = 8
PEER_NKEYS = 128
PEER_QDIM = 256
PEER_TOPK = 16
PAGE_SIZE = 128
EPS = 1e-6
NEG = -1e30

COL_Q = 0
COL_K = COL_Q + ATTN_WIDTH
COL_V = COL_K + ATTN_WIDTH
COL_QI = COL_V + ATTN_WIDTH
COL_SMALL = COL_QI + IDX_HEADS * IDX_DIM
COL_DQKV = 4096
COL_Z = COL_DQKV + 3 * DELTA_WIDTH
COL_END = COL_Z + DELTA_WIDTH
PROJ_PAD = COL_END
assert COL_SMALL + 128 <= COL_DQKV and COL_DQKV % DELTA_WIDTH == 0

LANES = 128
TOKEN_BLOCK = 512
ATTN_QBLOCK = 256
DELTA_CHUNK = 128
DELTA_HEADS_PER_STEP = 4
SCORE_PAGES_PER_STEP = 8
ATTEND_PAGES_PER_STEP = 8
PEER_EXPERT_BLOCK = 1024
VMEM_LIMIT = 56 * 1024 * 1024

_NT = (((1,), (1,)), ((), ()))


def _round_up(x, m):
    return (x + m - 1) // m * m


def _cparams(*sem):
    return pltpu.CompilerParams(dimension_semantics=sem, vmem_limit_bytes=VMEM_LIMIT)


def _norm_matmul_kernel(x_ref, g_ref, w_ref, o_ref, xn_ref):
    @pl.when(pl.program_id(1) == 0)
    def _():
        x = x_ref[...]
        ms = jnp.mean(x * x, axis=-1, keepdims=True)
        xn_ref[...] = (x * lax.rsqrt(ms + EPS) * g_ref[...]).astype(xn_ref.dtype)

    o_ref[...] = jnp.dot(xn_ref[...], w_ref[...], preferred_element_type=F32).astype(o_ref.dtype)


def norm_matmul(x, gain, w, *, tm, tn, out_dtype, return_normed):
    n, d = x.shape
    width = w.shape[1]
    grid = (n // tm, width // tn)
    in_specs = [
        pl.BlockSpec((tm, d), lambda i, j: (i, 0)),
        pl.BlockSpec((1, d), lambda i, j: (0, 0)),
        pl.BlockSpec((d, tn), lambda i, j: (0, j)),
    ]
    o_spec = pl.BlockSpec((tm, tn), lambda i, j: (i, j))
    o_shape = jax.ShapeDtypeStruct((n, width), out_dtype)
    xn_spec = pl.BlockSpec((tm, d), lambda i, j: (i, 0))
    if return_normed:
        return pl.pallas_call(
            _norm_matmul_kernel, grid=grid, in_specs=in_specs,
            out_specs=[o_spec, xn_spec],
            out_shape=[o_shape, jax.ShapeDtypeStruct((n, d), MXU_DTYPE)],
            compiler_params=_cparams("parallel", "arbitrary"), name="norm_matmul_xn",
        )(x, gain.reshape(1, d).astype(F32), w)
    return pl.pallas_call(
        _norm_matmul_kernel, grid=grid, in_specs=in_specs, out_specs=o_spec, out_shape=o_shape,
        scratch_shapes=[pltpu.VMEM((tm, d), MXU_DTYPE)],
        compiler_params=_cparams("parallel", "arbitrary"), name="norm_matmul",
    )(x, gain.reshape(1, d).astype(F32), w)


def _matmul2_kernel(a1_ref, a2_ref, w1_ref, w2_ref, r_ref, o_ref):
    acc = jnp.dot(a1_ref[...], w1_ref[...], preferred_element_type=F32)
    acc = acc + jnp.dot(a2_ref[...], w2_ref[...], preferred_element_type=F32)
    o_ref[...] = acc + r_ref[...]


def matmul2_residual(a1, a2, w1, w2, r, *, tm, tn):
    n, k1 = a1.shape
    k2 = a2.shape[1]
    d = w1.shape[1]
    return pl.pallas_call(
        _matmul2_kernel, grid=(n // tm, d // tn),
        in_specs=[
            pl.BlockSpec((tm, k1), lambda i, j: (i, 0)),
            pl.BlockSpec((tm, k2), lambda i, j: (i, 0)),
            pl.BlockSpec((k1, tn), lambda i, j: (0, j)),
            pl.BlockSpec((k2, tn), lambda i, j: (0, j)),
            pl.BlockSpec((tm, tn), lambda i, j: (i, j)),
        ],
        out_specs=pl.BlockSpec((tm, tn), lambda i, j: (i, j)),
        out_shape=jax.ShapeDtypeStruct((n, d), F32),
        compiler_params=_cparams("parallel", "arbitrary"), name="out_proj",
    )(a1, a2, w1, w2, r)


def _order_key(s):
    s = jnp.where(s == 0.0, 0.0, s)
    key = lax.bitcast_convert_type(s, jnp.int32)
    return jnp.where(key < 0, key ^ jnp.int32(0x7FFFFFFF), key)


def _count_lanes(mask):
    return jnp.sum(jnp.where(mask, 1.0, 0.0), axis=-1, keepdims=True)


def _count_pages_lanes(mask):
    per_query = jnp.sum(jnp.where(mask, 1.0, 0.0), axis=0, keepdims=True)
    return jnp.sum(per_query, axis=2, keepdims=True)


def _topk_threshold(key, col, topk, idx_bits, count):
    kf = float(topk)
    nonneg = count(key >= 0) >= kf
    base = jnp.where(nonneg, jnp.int32(0), jnp.int32(-2 ** 31))

    def value_bit(it, base):
        cand = base | jnp.left_shift(jnp.int32(1), 30 - it)
        return jnp.where(count(key >= cand) >= kf, cand, base)

    thr = lax.fori_loop(0, 31, value_bit, base)
    tied = key == thr
    need = kf - count(key > thr)

    def index_bit(it, j):
        cand = j | jnp.left_shift(jnp.int32(1), idx_bits - 1 - it)
        return jnp.where(count(tied & (col < cand)) < need, cand, j)

    j = lax.fori_loop(0, idx_bits, index_bit, jnp.zeros_like(thr))
    return thr, j


def _attn_prompt_kernel(q_ref, k_ref, v_ref, qi_ref, ki_ref, wi_ref, o_ref, *, q_start, topk, idx_bits):
    tq = q_ref.shape[1]
    kl = k_ref.shape[1]
    ki = ki_ref[0]
    wi = wi_ref[0]
    s = jnp.zeros((tq, kl), F32)
    for h in range(IDX_HEADS):
        d = lax.dot_general(qi_ref[0, h], ki, _NT, preferred_element_type=F32)
        s = s + jnp.maximum(d, 0.0) * wi[:, h:h + 1]
    row = q_start + lax.broadcasted_iota(jnp.int32, (tq, kl), 0)
    col = lax.broadcasted_iota(jnp.int32, (tq, kl), 1)
    causal = col <= row
    key = _order_key(jnp.where(causal, s, NEG))
    thr, j = _topk_threshold(key, col, topk, idx_bits, _count_lanes)
    sel = (key > thr) | ((key == thr) & (col <= j))
    bias = jnp.where(sel & causal, 0.0, NEG)
    scale = HEAD_DIM ** -0.5
    for h in range(ATTN_HEADS):
        hs = slice(h * HEAD_DIM, (h + 1) * HEAD_DIM)
        logits = lax.dot_general(q_ref[0, :, hs], k_ref[0, :, hs], _NT, preferred_element_type=F32) * scale + bias
        m = jnp.max(logits, axis=-1, keepdims=True)
        p = jnp.exp(logits - m)
        l = jnp.sum(p, axis=-1, keepdims=True)
        o = jnp.dot(p.astype(v_ref.dtype), v_ref[0, :, hs], preferred_element_type=F32)
        o_ref[0, :, hs] = (o / l).astype(o_ref.dtype)


def attn_prompt(qkv, qi, ki, wi, *, tq, topk):
    _, b, tp, w = qkv.shape
    outs = []
    for i in range(tp // tq):
        kl = (i + 1) * tq
        kern = functools.partial(_attn_prompt_kernel, q_start=i * tq, topk=topk,
                                 idx_bits=max(1, (kl - 1).bit_length()))
        outs.append(pl.pallas_call(
            kern, grid=(b,),
            in_specs=[
                pl.BlockSpec((None, 1, tq, w), lambda bb, i=i: (0, bb, i, 0)),
                pl.BlockSpec((None, 1, kl, w), lambda bb: (1, bb, 0, 0)),
                pl.BlockSpec((None, 1, kl, w), lambda bb: (2, bb, 0, 0)),
                pl.BlockSpec((1, IDX_HEADS, tq, IDX_DIM), lambda bb, i=i: (bb, 0, i, 0)),
                pl.BlockSpec((1, kl, IDX_DIM), lambda bb: (bb, 0, 0)),
                pl.BlockSpec((1, tq, IDX_HEADS), lambda bb, i=i: (bb, i, 0)),
            ],
            out_specs=pl.BlockSpec((1, tq, w), lambda bb: (bb, 0, 0)),
            out_shape=jax.ShapeDtypeStruct((b, tq, w), MXU_DTYPE),
            compiler_params=_cparams("parallel"), name=f"attn_prompt_q{i}",
        )(qkv, qkv, qkv, qi, ki, wi))
    return jnp.concatenate(outs, axis=1)


def _page_scores(qi, wcol, kpage, ds):
    d = lax.dot_general(qi, kpage.astype(qi.dtype), _NT, preferred_element_type=F32)
    r = jnp.maximum(d, 0.0) * wcol
    return jnp.sum(r.reshape(ds, IDX_HEADS, r.shape[-1]), axis=1)


def _sample_scores_kernel(pt_ref, qi_ref, wi_ref, knew_ref, *rest, n_pages_step, ds):
    page_refs = rest[:n_pages_step]
    past_ref, new_ref = rest[n_pages_step:]
    qi = qi_ref[0]
    wcol = wi_ref[0]
    for g in range(n_pages_step):
        past_ref[0, g] = _page_scores(qi, wcol, page_refs[g][0], ds)
    s_new = _page_scores(qi, wcol, knew_ref[0], ds)
    qrow = lax.broadcasted_iota(jnp.int32, s_new.shape, 0)
    kcol = lax.broadcasted_iota(jnp.int32, s_new.shape, 1)
    new_ref[0] = jnp.where(kcol <= qrow, s_new, NEG)


def sample_scores(qi, wi, ki_new_page, cache_kidx, page_table, *, ds):
    db, n_pages = page_table.shape
    g = math.gcd(SCORE_PAGES_PER_STEP, n_pages)
    rows = ds * IDX_HEADS
    page_specs = [pl.BlockSpec((1, PAGE_SIZE, IDX_DIM), lambda b, p, pt, gg=gg: (pt[b, p * g + gg], 0, 0))
                  for gg in range(g)]
    grid_spec = pltpu.PrefetchScalarGridSpec(
        num_scalar_prefetch=1, grid=(db, n_pages // g),
        in_specs=[pl.BlockSpec((1, rows, IDX_DIM), lambda b, p, pt: (b, 0, 0)),
                  pl.BlockSpec((1, rows, 1), lambda b, p, pt: (b, 0, 0)),
                  pl.BlockSpec((1, PAGE_SIZE, IDX_DIM), lambda b, p, pt: (b, 0, 0))] + page_specs,
        out_specs=[pl.BlockSpec((1, g, ds, PAGE_SIZE), lambda b, p, pt: (b, p, 0, 0)),
                   pl.BlockSpec((1, ds, PAGE_SIZE), lambda b, p, pt: (b, 0, 0))])
    return pl.pallas_call(
        functools.partial(_sample_scores_kernel, n_pages_step=g, ds=ds), grid_spec=grid_spec,
        out_shape=[jax.ShapeDtypeStruct((db, n_pages, ds, PAGE_SIZE), F32),
                   jax.ShapeDtypeStruct((db, ds, PAGE_SIZE), F32)],
        compiler_params=_cparams("parallel", "arbitrary"), name="sample_scores",
    )(page_table, qi, wi, ki_new_page, *([cache_kidx] * g))


def _sample_attend_kernel(pt_ref, sc_ref, scn_ref, q_ref, kn_ref, vn_ref, *rest, n_pages_step, ds, topk, idx_bits):
    k_refs = rest[:n_pages_step]
    v_refs = rest[n_pages_step:2 * n_pages_step]
    o_ref, thr_ref, j_ref, m_ref, l_ref, acc_ref = rest[2 * n_pages_step:]
    p = pl.program_id(1)
    n_pages = sc_ref.shape[1]
    nh = ATTN_HEADS
    scale = HEAD_DIM ** -0.5
    sub = lax.broadcasted_iota(jnp.int32, (nh, LANES), 0)
    lane = lax.broadcasted_iota(jnp.int32, (nh, LANES), 1)
    head_bias = jnp.where((lane < ds * nh) & (lane % nh == sub), 0.0, NEG)
    qrow = lax.broadcasted_iota(jnp.int32, (LANES, LANES), 0)
    qlane = lax.broadcasted_iota(jnp.int32, (LANES, LANES), 1)
    expand = jnp.where((qrow < ds) & (qlane // nh == qrow), 1.0, 0.0).astype(q_ref.dtype)

    @pl.when(p == 0)
    def _():
        key = _order_key(jnp.concatenate([sc_ref[0], scn_ref[0][None]], axis=0))
        col = (lax.broadcasted_iota(jnp.int32, key.shape, 0) * PAGE_SIZE
               + lax.broadcasted_iota(jnp.int32, key.shape, 2))
        thr, j = _topk_threshold(key, col, topk, idx_bits, _count_pages_lanes)
        thr_ref[...] = thr[0]
        j_ref[...] = j[0]
        m_ref[...] = jnp.full_like(m_ref, NEG)
        l_ref[...] = jnp.zeros_like(l_ref)
        acc_ref[...] = jnp.zeros_like(acc_ref)

    def attend(scores, page_idx, kpage, vpage):
        key = _order_key(scores)
        col = page_idx * PAGE_SIZE + lax.broadcasted_iota(jnp.int32, key.shape, 1)
        sel = ((key > thr_ref[...]) | ((key == thr_ref[...]) & (col <= j_ref[...]))) & (scores > 0.5 * NEG)
        sel_rows = jnp.concatenate([jnp.where(sel, 1.0, 0.0), jnp.zeros((LANES - ds, PAGE_SIZE), F32)], axis=0)
        spread = jnp.dot(sel_rows.T.astype(q_ref.dtype), expand, preferred_element_type=F32)
        key_bias = jnp.where(spread > 0.5, 0.0, NEG)
        bias = (key_bias[:, None, :] + head_bias[None, :, :]).reshape(PAGE_SIZE * nh, LANES)
        logits = jnp.dot(kpage.astype(q_ref.dtype), q_ref[0], preferred_element_type=F32) * scale + bias
        m_new = jnp.maximum(m_ref[...], jnp.max(logits, axis=0, keepdims=True))
        alpha = jnp.exp(m_ref[...] - m_new)
        pr = jnp.exp(logits - m_new)
        l_ref[...] = alpha * l_ref[...] + jnp.sum(pr, axis=0, keepdims=True)
        acc_ref[...] = alpha * acc_ref[...] + jnp.dot(vpage.T.astype(q_ref.dtype), pr.astype(q_ref.dtype),
                                                      preferred_element_type=F32)
        m_ref[...] = m_new

    for g in range(n_pages_step):
        page_idx = p * n_pages_step + g
        attend(sc_ref[0, page_idx], page_idx, k_refs[g][0], v_refs[g][0])

    @pl.when(p == pl.num_programs(1) - 1)
    def _():
        attend(scn_ref[0], n_pages, kn_ref[0], vn_ref[0])
        o_ref[0] = acc_ref[...] / jnp.where(l_ref[...] > 0.0, l_ref[...], 1.0)


def sample_attend(scores, scores_new, q_cols, k_new_page, v_new_page, cache_k, cache_v, page_table, *, ds, topk):
    db, n_pages = page_table.shape
    g = math.gcd(ATTEND_PAGES_PER_STEP, n_pages)
    page_rows = PAGE_SIZE * ATTN_HEADS
    idx_bits = max(1, ((n_pages + 1) * PAGE_SIZE - 1).bit_length())

    def page_spec(gg):
        return pl.BlockSpec((1, page_rows, HEAD_DIM), lambda b, p, pt, gg=gg: (pt[b, p * g + gg], 0, 0))

    grid_spec = pltpu.PrefetchScalarGridSpec(
        num_scalar_prefetch=1, grid=(db, n_pages // g),
        in_specs=[pl.BlockSpec((1, n_pages, ds, PAGE_SIZE), lambda b, p, pt: (b, 0, 0, 0)),
                  pl.BlockSpec((1, ds, PAGE_SIZE), lambda b, p, pt: (b, 0, 0)),
                  pl.BlockSpec((1, HEAD_DIM, LANES), lambda b, p, pt: (b, 0, 0)),
                  pl.BlockSpec((1, page_rows, HEAD_DIM), lambda b, p, pt: (b, 0, 0)),
                  pl.BlockSpec((1, page_rows, HEAD_DIM), lambda b, p, pt: (b, 0, 0))]
        + [page_spec(gg) for gg in range(g)] * 2,
        out_specs=pl.BlockSpec((1, HEAD_DIM, LANES), lambda b, p, pt: (b, 0, 0)),
        scratch_shapes=[pltpu.VMEM((ds, 1), jnp.int32), pltpu.VMEM((ds, 1), jnp.int32),
                        pltpu.VMEM((1, LANES), F32), pltpu.VMEM((1, LANES), F32), pltpu.VMEM((HEAD_DIM, LANES), F32)])
    kern = functools.partial(_sample_attend_kernel, n_pages_step=g, ds=ds, topk=topk, idx_bits=idx_bits)
    return pl.pallas_call(
        kern, grid_spec=grid_spec, out_shape=jax.ShapeDtypeStruct((db, HEAD_DIM, LANES), F32),
        compiler_params=_cparams("parallel", "arbitrary"), name="sample_attend",
    )(page_table, scores, scores_new, q_cols, k_new_page, v_new_page, *([cache_k] * g), *([cache_v] * g))


def _split_hi_lo(x):
    hi = x.astype(MXU_DTYPE)
    return hi, (x - hi.astype(F32)).astype(MXU_DTYPE)


def _dot1(a, b, dims=None):
    if dims is None:
        dims = (((1,), (0,)), ((), ()))
    return lax.dot_general(a.astype(MXU_DTYPE), b.astype(MXU_DTYPE), dims, preferred_element_type=F32)


def _dot3(a, b, dims=None):
    ah, al = _split_hi_lo(a)
    bh, bl = _split_hi_lo(b)
    if dims is None:
        dims = (((1,), (0,)), ((), ()))
    d = functools.partial(lax.dot_general, dimension_numbers=dims, preferred_element_type=F32)
    return d(ah, bh) + (d(ah, bl) + d(al, bh))


def _delta_kernel(q_ref, k_ref, v_ref, z_ref, gcum_ref, beta_ref, s0_ref, dn_ref, o_ref, s_ref, t_scr, qk_scr,
                  *, n_chunks):
    c_len = DELTA_CHUNK
    n_h = s0_ref.shape[1]
    ri = lax.broadcasted_iota(jnp.int32, (c_len, c_len), 0)
    ci = lax.broadcasted_iota(jnp.int32, (c_len, c_len), 1)
    incl = ri >= ci
    strict = ri > ci
    eye = jnp.where(ri == ci, 1.0, 0.0)
    dn = dn_ref[...]

    def chunk_rows(c):
        return pl.ds(pl.multiple_of(c * c_len, c_len), c_len)

    def decay_terms(c, hh):
        g_row = jnp.broadcast_to(gcum_ref[0, hh, pl.ds(c, 1), :], (c_len, c_len))
        b_col = jnp.broadcast_to(beta_ref[0, hh, pl.ds(c, 1), :], (c_len, c_len)).T
        return g_row, g_row.T, b_col

    heads = range(n_h)
    head_cols = [slice(hh * DELTA_DK, (hh + 1) * DELTA_DK) for hh in heads]

    def build(c, carry):
        rows = chunk_rows(c)
        qc = [q_ref[0, rows, cs] for cs in head_cols]
        kc = [k_ref[0, rows, cs] for cs in head_cols]
        terms = [decay_terms(c, hh) for hh in heads]
        dec = [jnp.where(incl, jnp.exp(jnp.where(incl, g_col - g_row, 0.0)), 0.0) for g_row, g_col, _ in terms]
        kk = [_dot3(kc[hh], kc[hh], _NT) for hh in heads]
        a = [jnp.where(strict, kk[hh] * dec[hh], 0.0) * terms[hh][2] for hh in heads]
        t = [eye - a[hh] for hh in heads]
        pw = [_dot1(a[hh], a[hh]) for hh in heads]
        span = 2
        while span < c_len:
            t = [t[hh] + _dot1(t[hh], pw[hh]) for hh in heads]
            span *= 2
            if span < c_len:
                pw = [_dot1(pw[hh], pw[hh]) for hh in heads]
        at = [_dot3(a[hh], t[hh]) for hh in heads]
        corr = [_dot3(t[hh], eye - t[hh] - at[hh]) for hh in heads]
        qk = [_dot1(qc[hh], kc[hh], _NT) for hh in heads]
        for hh in heads:
            t_scr[hh, c] = t[hh] + corr[hh]
            qk_scr[hh, c] = qk[hh] * dec[hh]
        return carry

    lax.fori_loop(0, n_chunks, build, 0)

    def step(c, states):
        rows = chunk_rows(c)
        qc = [q_ref[0, rows, cs] for cs in head_cols]
        kc = [k_ref[0, rows, cs] for cs in head_cols]
        terms = [decay_terms(c, hh) for hh in heads]
        e_g = [jnp.exp(g_col) for _, g_col, _ in terms]
        ks = [_dot1(kc[hh], states[hh]) for hh in heads]
        qs = [_dot1(qc[hh], states[hh]) for hh in heads]
        u = [_dot3(t_scr[hh, c], terms[hh][2] * (v_ref[0, rows, head_cols[hh]] - e_g[hh] * ks[hh])) for hh in heads]
        qku = [_dot1(qk_scr[hh, c], u[hh]) for hh in heads]
        g_last = [g_row[:, c_len - 1:c_len] for g_row, _, _ in terms]
        kw = [kc[hh] * jnp.exp(g_last[hh] - terms[hh][1]) for hh in heads]
        upd = [_dot1(kw[hh].T, u[hh]) for hh in heads]
        for hh in heads:
            o = e_g[hh] * qs[hh] + qku[hh]
            z = z_ref[0, rows, head_cols[hh]]
            on = o * lax.rsqrt(jnp.mean(o * o, axis=-1, keepdims=True) + EPS) * dn
            o_ref[0, rows, head_cols[hh]] = (on * (z / (1.0 + jnp.exp(-z)))).astype(o_ref.dtype)
        return tuple(jnp.exp(g_last[hh]) * states[hh] + upd[hh] for hh in heads)

    final = lax.fori_loop(0, n_chunks, step, tuple(s0_ref[0, hh] for hh in range(n_h)))
    for hh in range(n_h):
        s_ref[0, hh] = final[hh]


def gated_delta(qkvz, gcum, beta, s0, delta_norm, *, n_chunks):
    _, b, tp, _ = qkvz.shape
    h = DELTA_HEADS
    n_h = DELTA_HEADS_PER_STEP
    tok = pl.BlockSpec((1, tp, n_h * DELTA_DV), lambda bb, hh: (bb, 0, hh))
    ops = [pl.BlockSpec((None, 1, tp, n_h * DELTA_DV), lambda bb, hh, g=g: (g, bb, 0, hh)) for g in range(4)]
    row = pl.BlockSpec((1, n_h, n_chunks, DELTA_CHUNK), lambda bb, hh: (bb, hh, 0, 0))
    st = pl.BlockSpec((1, n_h, DELTA_DK, DELTA_DV), lambda bb, hh: (bb, hh, 0, 0))
    mat = pltpu.VMEM((n_h, n_chunks, DELTA_CHUNK, DELTA_CHUNK), F32)
    return pl.pallas_call(
        functools.partial(_delta_kernel, n_chunks=n_chunks), grid=(b, h // n_h),
        in_specs=ops + [row, row, st, pl.BlockSpec((1, DELTA_DV), lambda bb, hh: (0, 0))],
        out_specs=[tok, st],
        out_shape=[jax.ShapeDtypeStruct((b, tp, h * DELTA_DV), MXU_DTYPE),
                   jax.ShapeDtypeStruct((b, h, DELTA_DK, DELTA_DV), F32)],
        scratch_shapes=[mat, mat],
        compiler_params=_cparams("parallel", "parallel"), name="gated_delta",
    )(qkvz, qkvz, qkvz, qkvz, gcum, beta, s0, delta_norm.reshape(1, DELTA_DV).astype(F32))


def _top_values(cur, n):
    vals = []
    for r in range(n):
        m = jnp.max(cur, axis=0, keepdims=True)
        vals.append(m)
        if r + 1 < n:
            cur = jnp.where(cur == m, -jnp.inf, cur)
    return vals


_PEER_RANKS = PEER_TOPK + 1
_PEER_PAIRS = [(r, c) for r in range(_PEER_RANKS) for c in range(_PEER_RANKS) if (r + 1) * (c + 1) <= _PEER_RANKS]


def _peer_select_kernel(q_ref, sk_ref, a_ref, b_ref, misc_ref):
    tn = q_ref.shape[1]

    def head(h, carry):
        shifted = []
        tops = []
        for c in range(2):
            st = lax.dot_general(sk_ref[c], q_ref[h * 2 + c], _NT, preferred_element_type=F32)
            vals = _top_values(st, _PEER_RANKS)
            shifted.append(st - vals[0])
            tops.append([v - vals[0] for v in vals])
        a_ref[h] = shifted[0]
        b_ref[h] = shifted[1]
        cand = [tops[0][r] + tops[1][c] for r, c in _PEER_PAIRS]
        pad = _round_up(len(cand), 8) - len(cand)
        cand = jnp.concatenate(cand + [jnp.full((pad, tn), -jnp.inf, F32)], axis=0)
        best = _top_values(cand, _PEER_RANKS)
        zsum = jnp.zeros_like(best[0])
        for v in best[:PEER_TOPK]:
            zsum = zsum + jnp.exp(v)
        thr = 0.5 * (best[PEER_TOPK - 1] + best[PEER_TOPK])
        misc_ref[h] = jnp.concatenate([thr, 1.0 / zsum, jnp.zeros((6, tn), F32)], axis=0)
        return carry

    lax.fori_loop(0, PEER_HEADS, head, 0)


def peer_select(q, subkeys, *, tn):
    n = q.shape[1]
    blk = pl.BlockSpec((PEER_HEADS, PEER_NKEYS, tn), lambda i: (0, 0, i))
    return pl.pallas_call(
        _peer_select_kernel, grid=(n // tn,),
        in_specs=[pl.BlockSpec((q.shape[0], tn, q.shape[2]), lambda i: (0, i, 0)),
                  pl.BlockSpec(subkeys.shape, lambda i: (0, 0, 0))],
        out_specs=[blk, blk, pl.BlockSpec((PEER_HEADS, 8, tn), lambda i: (0, 0, i))],
        out_shape=[jax.ShapeDtypeStruct((PEER_HEADS, PEER_NKEYS, n), F32),
                   jax.ShapeDtypeStruct((PEER_HEADS, PEER_NKEYS, n), F32),
                   jax.ShapeDtypeStruct((PEER_HEADS, 8, n), F32)],
        compiler_params=_cparams("parallel"), name="peer_select",
    )(q, subkeys)


def _gelu_tanh(x):
    return 0.5 * x * (1.0 + jnp.tanh(math.sqrt(2.0 / math.pi) * (x + 0.044715 * (x * x * x))))


def _peer_dense_kernel(xn_ref, a_ref, b_ref, misc_ref, u_ref, vt_ref, x_ref, gf_ref, o_ref,
                       acc_ref, bexp_ref, g_ref, row_ref, ht_ref):
    e = pl.program_id(1)
    te = u_ref.shape[0]
    tn = xn_ref.shape[0]
    nk = PEER_NKEYS

    @pl.when(e == 0)
    def _():
        acc_ref[...] = jnp.zeros_like(acc_ref)
        for h in range(PEER_HEADS):
            bexp_ref[h] = jnp.exp(b_ref[h])

    n_i = te // nk
    col_w = min(tn, 2 * LANES)
    n_col = tn // col_w

    def up_piece(c):
        cs = slice(c * col_w, (c + 1) * col_w)
        ht_ref[:, cs] = lax.dot_general(u_ref[...], xn_ref[cs, :], _NT, preferred_element_type=F32)

    def down_piece(c):
        cs = slice(c * col_w, (c + 1) * col_w)
        acc_ref[:, cs] += jnp.dot(vt_ref[...], g_ref[:, cs], preferred_element_type=F32)

    thr_all = jnp.concatenate([misc_ref[h, 0:1, :] for h in range(PEER_HEADS)], axis=0)
    invz_all = jnp.concatenate([misc_ref[h, 1:2, :] for h in range(PEER_HEADS)], axis=0)
    for ii in range(n_i):
        a_rows = jnp.concatenate([a_ref[h, pl.ds(e * n_i + ii, 1), :] for h in range(PEER_HEADS)], axis=0)
        row_ref[0, ii] = thr_all - a_rows
        row_ref[1, ii] = jnp.exp(a_rows) * invz_all

    key_tile = 16

    def gate_columns(c):
        for tb in range(c * col_w // LANES, (c + 1) * col_w // LANES):
            lanes = slice(tb * LANES, (tb + 1) * LANES)
            for kt in range(nk // key_tile):
                keys = slice(kt * key_tile, (kt + 1) * key_tile)
                w = [None] * n_i
                for h in range(PEER_HEADS):
                    b_tile = b_ref[h, keys, lanes]
                    bexp_tile = bexp_ref[h, keys, lanes]
                    for ii in range(n_i):
                        wh = (jnp.where(b_tile >= row_ref[0, ii, h:h + 1, lanes], bexp_tile, 0.0)
                              * row_ref[1, ii, h:h + 1, lanes])
                        w[ii] = wh if w[ii] is None else w[ii] + wh
                for ii in range(n_i):
                    rows = slice(ii * nk + kt * key_tile, ii * nk + (kt + 1) * key_tile)
                    g_ref[rows, lanes] = (w[ii] * _gelu_tanh(ht_ref[rows, lanes])).astype(g_ref.dtype)

    up_piece(0)
    for c in range(n_col):
        if c + 1 < n_col:
            up_piece(c + 1)
        gate_columns(c)
        down_piece(c)

    @pl.when(e == pl.num_programs(1) - 1)
    def _():
        y = acc_ref[...].T + x_ref[...]
        ms = jnp.mean(y * y, axis=-1, keepdims=True)
        o_ref[...] = y * lax.rsqrt(ms + EPS) * gf_ref[...]


def peer_dense(xn, a, b, misc, u, vt, x, gain_final, *, tn, te):
    n, d = x.shape
    n_blocks = u.shape[0] // te
    once = pl.Buffered(1)
    sel = pl.BlockSpec((PEER_HEADS, PEER_NKEYS, tn), lambda i, e: (0, 0, i), pipeline_mode=once)
    return pl.pallas_call(
        _peer_dense_kernel, grid=(n // tn, n_blocks),
        in_specs=[
            pl.BlockSpec((tn, d), lambda i, e: (i, 0), pipeline_mode=once),
            sel, sel,
            pl.BlockSpec((PEER_HEADS, 8, tn), lambda i, e: (0, 0, i)),
            pl.BlockSpec((te, d), lambda i, e: (e, 0)),
            pl.BlockSpec((d, te), lambda i, e: (0, e)),
            pl.BlockSpec((tn, d), lambda i, e: (i, 0), pipeline_mode=once),
            pl.BlockSpec((1, d), lambda i, e: (0, 0)),
        ],
        out_specs=pl.BlockSpec((tn, d), lambda i, e: (i, 0)),
        out_shape=jax.ShapeDtypeStruct((n, d), F32),
        scratch_shapes=[pltpu.VMEM((d, tn), F32),
                        pltpu.VMEM((PEER_HEADS, PEER_NKEYS, tn), F32),
                        pltpu.VMEM((te, tn), MXU_DTYPE),
                        pltpu.VMEM((2, te // PEER_NKEYS, PEER_HEADS, tn), F32),
                        pltpu.VMEM((te, tn), F32)],
        compiler_params=_cparams("parallel", "arbitrary"), name="peer_dense",
    )(xn, a, b, misc, u, vt, x, gain_final.reshape(1, d).astype(F32))


PREP_LANES = 256


def _rope_cast_kernel(x_ref, c_ref, s1_ref, s2_ref, obf_ref, of_ref, *, n_rot_groups):
    t = x_ref.shape[1]
    tp = obf_ref.shape[2]
    x = x_ref[0]

    def emit(y):
        of_ref[0, 0] = y
        obf_ref[0, 0, :t, :] = y.astype(obf_ref.dtype)
        if tp > t:
            obf_ref[0, 0, t:, :] = jnp.zeros((tp - t, y.shape[1]), obf_ref.dtype)

    @pl.when(pl.program_id(1) < n_rot_groups)
    def _():
        parts = []
        for h in range(x.shape[1] // HEAD_DIM):
            xs = x[:, h * HEAD_DIM:(h + 1) * HEAD_DIM]
            rot = HEAD_DIM // 8
            parts.append(xs * c_ref[...] + pltpu.roll(xs, rot, 1) * s1_ref[...]
                         + pltpu.roll(xs, HEAD_DIM - rot, 1) * s2_ref[...])
        emit(jnp.concatenate(parts, axis=1))

    @pl.when(pl.program_id(1) >= n_rot_groups)
    def _():
        emit(x)


def rope_cast(p, pos, *, tp):
    b, t, _ = p.shape
    rot = HEAD_DIM // 4
    half = rot // 2
    inv = ROPE_THETA ** (-jnp.arange(half, dtype=F32) * 2.0 / rot)
    ang = pos.astype(F32)[:, None] * inv[None, :]
    cos, sin, zero = jnp.cos(ang), jnp.sin(ang), jnp.zeros((t, half), F32)
    rest = HEAD_DIM - rot
    c = jnp.concatenate([cos, cos, jnp.ones((t, rest), F32)], axis=1)
    s1 = jnp.concatenate([zero, sin, jnp.zeros((t, rest), F32)], axis=1)
    s2 = jnp.concatenate([-sin, zero, jnp.zeros((t, rest), F32)], axis=1)
    per_group = ATTN_WIDTH // PREP_LANES
    tab = pl.BlockSpec((t, HEAD_DIM), lambda bb, g, j: (0, 0))
    return pl.pallas_call(
        functools.partial(_rope_cast_kernel, n_rot_groups=2), grid=(b, 3, per_group),
        in_specs=[pl.BlockSpec((1, t, PREP_LANES), lambda bb, g, j: (bb, 0, g * per_group + j)), tab, tab, tab],
        out_specs=[pl.BlockSpec((1, 1, tp, PREP_LANES), lambda bb, g, j: (g, bb, 0, j)),
                   pl.BlockSpec((1, 1, t, PREP_LANES), lambda bb, g, j: (g, bb, 0, j))],
        out_shape=[jax.ShapeDtypeStruct((3, b, tp, ATTN_WIDTH), MXU_DTYPE),
                   jax.ShapeDtypeStruct((3, b, t, ATTN_WIDTH), F32)],
        compiler_params=_cparams("parallel", "parallel", "parallel"), name="rope_cast",
    )(p, c, s1, s2)


def _delta_prep_kernel(x_ref, prev_ref, w_ref, o_ref):
    g = pl.program_id(1)
    t = x_ref.shape[1]
    tp = o_ref.shape[2]
    x = x_ref[0]

    def emit(y):
        o_ref[0, 0, :t, :] = y
        if tp > t:
            o_ref[0, 0, t:, :] = jnp.zeros((tp - t, y.shape[1]), F32)

    def conv_silu():
        w = w_ref[...]
        head = jnp.concatenate([prev_ref[0], x[:8]], axis=0)
        acc = x * w[CONV_WIDTH - 1:CONV_WIDTH]
        for s in range(1, CONV_WIDTH):
            shifted = jnp.concatenate([head[8 - s:16 - s], pltpu.roll(x, s, 0)[8:]], axis=0)
            acc = acc + shifted * w[CONV_WIDTH - 1 - s:CONV_WIDTH - s]
        return acc * (1.0 / (1.0 + jnp.exp(-acc)))

    @pl.when(g < 2)
    def _():
        y = conv_silu()
        scale = jnp.where(g == 0, DELTA_DK ** -0.5, 1.0)
        parts = []
        for h in range(y.shape[1] // DELTA_DK):
            ys = y[:, h * DELTA_DK:(h + 1) * DELTA_DK]
            parts.append(ys * lax.rsqrt(jnp.sum(ys * ys, axis=-1, keepdims=True) + EPS) * scale)
        emit(jnp.concatenate(parts, axis=1))

    @pl.when(g == 2)
    def _():
        emit(conv_silu())

    @pl.when(g == 3)
    def _():
        emit(x)


def delta_prep(p, conv_prev, conv_w, *, tp):
    b, t, _ = p.shape
    assert t >= 8
    per_group = DELTA_WIDTH // PREP_LANES
    first = COL_DQKV // PREP_LANES
    prev8 = jnp.pad(conv_prev.astype(F32), ((0, 0), (8 - (CONV_WIDTH - 1), 0), (0, 0)))

    def conv_col(g, j):
        return jnp.minimum(g, 2) * per_group + j

    return pl.pallas_call(
        _delta_prep_kernel, grid=(b, 4, per_group),
        in_specs=[pl.BlockSpec((1, t, PREP_LANES), lambda bb, g, j: (bb, 0, first + g * per_group + j)),
                  pl.BlockSpec((1, 8, PREP_LANES), lambda bb, g, j: (bb, 0, conv_col(g, j))),
                  pl.BlockSpec((CONV_WIDTH, PREP_LANES), lambda bb, g, j: (0, conv_col(g, j)))],
        out_specs=pl.BlockSpec((1, 1, tp, PREP_LANES), lambda bb, g, j: (g, bb, 0, j)),
        out_shape=jax.ShapeDtypeStruct((4, b, tp, DELTA_WIDTH), F32),
        compiler_params=_cparams("parallel", "parallel", "parallel"), name="delta_prep",
    )(p, prev8, conv_w.astype(F32))


def _rope(x, pos):
    rot = x.shape[-1] // 4
    half = rot // 2
    inv = ROPE_THETA ** (-jnp.arange(half, dtype=F32) * 2.0 / rot)
    ang = pos.astype(F32)[:, None] * inv[None, :]
    cos = jnp.cos(ang)[None, :, None, :]
    sin = jnp.sin(ang)[None, :, None, :]
    x1 = x[..., :half]
    x2 = x[..., half:rot]
    return jnp.concatenate([x1 * cos - x2 * sin, x2 * cos + x1 * sin, x[..., rot:]], axis=-1)


def _l2norm(x):
    return x * lax.rsqrt(jnp.sum(x * x, axis=-1, keepdims=True) + EPS)


def _small_projection(p, pos, a_log, dt_bias):
    b, t, _ = p.shape
    qi = _rope(p[..., COL_QI:COL_SMALL].reshape(b, t, IDX_HEADS, IDX_DIM), pos)
    ki = _rope(p[..., COL_SMALL:COL_SMALL + IDX_DIM][:, :, None, :], pos)[:, :, 0, :]
    o = COL_SMALL + IDX_DIM
    wi = p[..., o:o + IDX_HEADS] * (IDX_HEADS * IDX_DIM) ** -0.5
    beta = jax.nn.sigmoid(p[..., o + IDX_HEADS:o + IDX_HEADS + DELTA_HEADS])
    a = p[..., o + IDX_HEADS + DELTA_HEADS:o + IDX_HEADS + 2 * DELTA_HEADS]
    g = -jnp.exp(a_log.astype(F32)) * jax.nn.softplus(a + dt_bias.astype(F32))
    return qi, ki, wi, g, beta


def _split_projection(p, pos, conv_prev, conv_w, a_log, dt_bias):
    b, t, _ = p.shape
    q = _rope(p[..., COL_Q:COL_K].reshape(b, t, ATTN_HEADS, HEAD_DIM), pos)
    k = _rope(p[..., COL_K:COL_V].reshape(b, t, ATTN_HEADS, HEAD_DIM), pos)
    v = p[..., COL_V:COL_QI].reshape(b, t, ATTN_HEADS, HEAD_DIM)
    qi, ki, wi, g, beta = _small_projection(p, pos, a_log, dt_bias)
    dqkv = p[..., COL_DQKV:COL_Z]
    z = p[..., COL_Z:COL_END]
    xpad = jnp.concatenate([conv_prev.astype(F32), dqkv], axis=1)
    conv = sum(conv_w[i] * xpad[:, i:i + t] for i in range(CONV_WIDTH))
    conv = jax.nn.silu(conv)
    new_conv = xpad[:, t:]
    dq = _l2norm(conv[..., :DELTA_WIDTH].reshape(b, t, DELTA_HEADS, DELTA_DK)) * DELTA_DK ** -0.5
    dk = _l2norm(conv[..., DELTA_WIDTH:2 * DELTA_WIDTH].reshape(b, t, DELTA_HEADS, DELTA_DK))
    dv = conv[..., 2 * DELTA_WIDTH:]
    return (q, k, v, qi, ki, wi), (dq.reshape(b, t, DELTA_WIDTH), dk.reshape(b, t, DELTA_WIDTH), dv, g, beta, z), new_conv


def _pad_rows(x, tp):
    return jnp.pad(x, [(0, 0), (0, tp - x.shape[1])] + [(0, 0)] * (x.ndim - 2))


def _delta_rows(t):
    n_chunks = -(-t // DELTA_CHUNK)
    return n_chunks, n_chunks * DELTA_CHUNK


def _delta_group(qkvz, g, beta, s0, delta_norm):
    b, t, _ = g.shape
    n_chunks, tp = _delta_rows(t)

    def rows(x):
        return _pad_rows(x, tp).reshape(b, n_chunks, DELTA_CHUNK, DELTA_HEADS).transpose(0, 3, 1, 2)

    gcum = jnp.cumsum(rows(g), axis=-1)
    o, s = gated_delta(qkvz, gcum, rows(beta), s0.astype(F32), delta_norm, n_chunks=n_chunks)
    return o[:, :t], s


def _sample_attention(q, k, v, qi, ki, wi, cache_k, cache_v, cache_kidx, page_table, topk):
    db, ds, h, hd = q.shape
    n_pool = cache_k.shape[0]

    def new_page(x):
        return _pad_rows(x.reshape(db, ds, -1), PAGE_SIZE)

    scores, scores_new = sample_scores(
        qi.reshape(db, ds * IDX_HEADS, IDX_DIM).astype(MXU_DTYPE), wi.reshape(db, ds * IDX_HEADS, 1),
        new_page(ki), cache_kidx, page_table, ds=ds)
    q_cols = jnp.pad(q.reshape(db, ds * h, hd).transpose(0, 2, 1), ((0, 0), (0, 0), (0, LANES - ds * h)))

    def key_head_rows(x):
        return x.reshape(x.shape[:-3] + (x.shape[-3] * h, hd))

    o = sample_attend(scores, scores_new, q_cols.astype(MXU_DTYPE),
                      key_head_rows(_pad_rows(k, PAGE_SIZE)), key_head_rows(_pad_rows(v, PAGE_SIZE)),
                      key_head_rows(cache_k), key_head_rows(cache_v), page_table, ds=ds, topk=topk)
    return o[:, :, :ds * h].transpose(0, 2, 1).reshape(db, ds, h * hd)


def _token_tile(n, want):
    if n % LANES:
        return n
    best = LANES
    for m in range(LANES, want + 1, LANES):
        if n % m == 0:
            best = m
    return best


def _layer(l, x, x_groups, dims, cache_k, cache_v, cache_kidx, state_conv, state_ssm, page_table,
           norm_mix, w_in, conv_w, a_log, dt_bias, delta_norm, w_out, norm_ffn,
           peer_wq, peer_subkeys, peer_u, peer_v, norm_out):
    b, t, db, ds = dims
    n_pad, d = x.shape
    n_p, n_s = b * t, db * ds
    past = page_table.shape[1] * PAGE_SIZE
    tm = _token_tile(n_pad, 1280)

    w = w_in[l]
    o_wi = 3 * ATTN_WIDTH + IDX_HEADS * IDX_DIM + IDX_DIM + IDX_HEADS
    o_z = o_wi + 3 * DELTA_WIDTH
    o_b = o_z + DELTA_WIDTH
    w_perm = jnp.concatenate(
        [w[:, :o_wi], w[:, o_b:o_b + 2 * DELTA_HEADS],
         jnp.zeros((d, COL_DQKV - (o_wi + 2 * DELTA_HEADS)), w.dtype), w[:, o_wi:o_b]], axis=1).astype(MXU_DTYPE)
    p_p = norm_matmul(x_groups[0], norm_mix[l], w_perm, tm=_token_tile(n_p, 1280), tn=1024, out_dtype=F32,
                      return_normed=False).reshape(b, t, PROJ_PAD)
    p_s = norm_matmul(x_groups[1], norm_mix[l], w_perm, tm=_token_tile(n_s, 1280), tn=1024, out_dtype=F32,
                      return_normed=False).reshape(db, ds, PROJ_PAD)

    pos_p = jnp.arange(t)
    tp = _round_up(t, ATTN_QBLOCK)
    topk_p = min(TOPK_MAX, (t - N_META) // 4)
    qkv, qkv_f32 = rope_cast(p_p, pos_p, tp=tp)
    k_p = qkv_f32[1].reshape(b, t, ATTN_HEADS, HEAD_DIM)
    v_p = qkv_f32[2].reshape(b, t, ATTN_HEADS, HEAD_DIM)
    qi, ki_p, wi, g, beta = _small_projection(p_p, pos_p, a_log[l], dt_bias[l])
    attn_p = attn_prompt(qkv, _pad_rows(qi, tp).transpose(0, 2, 1, 3).astype(MXU_DTYPE),
                         _pad_rows(ki_p, tp).astype(MXU_DTYPE), _pad_rows(wi, tp),
                         tq=ATTN_QBLOCK, topk=topk_p)[:, :t]
    conv_prev_p = jnp.zeros((b, CONV_WIDTH - 1, 3 * DELTA_WIDTH), F32)
    qkvz = delta_prep(p_p, conv_prev_p, conv_w[l], tp=_delta_rows(t)[1])
    conv_p = jnp.concatenate([conv_prev_p, p_p[:, -(CONV_WIDTH - 1):, COL_DQKV:COL_Z]], axis=1)[:, -(CONV_WIDTH - 1):]
    gated_p, ssm_p = _delta_group(qkvz, g, beta, jnp.zeros((b, DELTA_HEADS, DELTA_DK, DELTA_DV), F32), delta_norm[l])

    pos_s = past + jnp.arange(ds)
    (q, k_s, v_s, qi, ki_s, wi), (dq, dk, dv, g, beta, z), conv_s = _split_projection(
        p_s, pos_s, state_conv[l], conv_w[l], a_log[l], dt_bias[l])
    topk_s = min(TOPK_MAX, (past + ds) // 4)
    attn_s = _sample_attention(q, k_s, v_s, qi, ki_s, wi, cache_k[l], cache_v[l], cache_kidx[l],
                               page_table, topk_s).astype(MXU_DTYPE)
    tp_s = _delta_rows(ds)[1]
    gated_s, ssm_s = _delta_group(jnp.stack([_pad_rows(a_, tp_s) for a_ in (dq, dk, dv, z)]), g, beta,
                                  state_ssm[l], delta_norm[l])

    def flat(xp_, xs_):
        return jnp.concatenate([xp_.reshape(n_p, -1), xs_.reshape(n_s, -1),
                                jnp.zeros((n_pad - n_p - n_s, xp_.shape[-1]), xp_.dtype)], axis=0)

    wo = w_out[l].astype(MXU_DTYPE)
    x1 = matmul2_residual(flat(attn_p, attn_s), flat(gated_p, gated_s), wo[:ATTN_WIDTH], wo[ATTN_WIDTH:], x,
                          tm=tm, tn=1024 if d % 1024 == 0 else d)

    wq = peer_wq[l].astype(MXU_DTYPE)
    qp, xn = norm_matmul(x1, norm_ffn[l], wq, tm=tm, tn=wq.shape[1], out_dtype=MXU_DTYPE, return_normed=True)
    qp = qp.reshape(n_pad, 2 * PEER_HEADS, PEER_QDIM // 2).transpose(1, 0, 2)
    a, bb, misc = peer_select(qp, peer_subkeys[l].astype(MXU_DTYPE), tn=TOKEN_BLOCK)
    y = peer_dense(xn, a, bb, misc, peer_u[l].astype(MXU_DTYPE), peer_v[l].T.astype(MXU_DTYPE), x1, norm_out,
                   tn=TOKEN_BLOCK, te=PEER_EXPERT_BLOCK)
    caches_p = (k_p, v_p, ki_p, conv_p, ssm_p)
    caches_s = (k_s, v_s, ki_s, conv_s, ssm_s)
    return y, caches_p, caches_s


def kernel(x_prompt, x_sample, cache_k, cache_v, cache_kidx, state_conv, state_ssm, page_table, meta, norm_mix,
           w_in, conv_w, a_log, dt_bias, delta_norm, w_out, norm_ffn, peer_wq, peer_subkeys, peer_u, peer_v,
           norm_final):
    b, s, d = x_prompt.shape
    db, ds, _ = x_sample.shape
    t = s + N_META
    depth = w_in.shape[0]
    assert depth == 1, "the fused PEER + final-norm epilogue assumes a single layer"
    n_p, n_s = b * t, db * ds
    n_pad = _round_up(n_p + n_s, TOKEN_BLOCK)
    xp = jnp.concatenate([jnp.broadcast_to(meta[None].astype(F32), (b, N_META, d)), x_prompt], axis=1)
    x = jnp.concatenate([xp.reshape(n_p, d), x_sample.reshape(n_s, d), jnp.zeros((n_pad - n_p - n_s, d), F32)], axis=0)
    y, cp, cs = _layer(0, x, (xp.reshape(n_p, d), x_sample.reshape(n_s, d)), (b, t, db, ds), cache_k, cache_v, cache_kidx, state_conv, state_ssm, page_table,
                       norm_mix, w_in, conv_w, a_log, dt_bias, delta_norm, w_out, norm_ffn,
                       peer_wq, peer_subkeys, peer_u, peer_v, norm_final)
    y_prompt = y[:n_p].reshape(b, t, d)[:, N_META:]
    y_sample = y[n_p:n_p + n_s].reshape(db, ds, d)
    return (y_prompt, y_sample) + tuple(c[None] for c in cp) + tuple(c[None] for c in cs)
```

```python
import functools
import math

import jax
import jax.numpy as jnp
from jax import lax
from jax.experimental import pallas as pl
from jax.experimental.pallas import tpu as pltpu

F32 = jnp.float32
MXU_DTYPE = jnp.bfloat16

N_META = 16
HEAD_DIM = 128
ATTN_HEADS = 8
ATTN_WIDTH = ATTN_HEADS * HEAD_DIM
IDX_HEADS = 8
IDX_DIM = 64
TOPK_MAX = 256
ROPE_THETA = 500000.0
DELTA_DK = 128
DELTA_DV = 128
DELTA_HEADS = 8
DELTA_WIDTH = DELTA_HEADS * DELTA_DV
CONV_WIDTH = 4
PEER_HEADS = 8
PEER_NKEYS = 128
PEER_QDIM = 256
PEER_TOPK = 16
PAGE_SIZE = 128
EPS = 1e-6
NEG = -1e30

COL_Q = 0
COL_K = COL_Q + ATTN_WIDTH
COL_V = COL_K + ATTN_WIDTH
COL_QI = COL_V + ATTN_WIDTH
COL_SMALL = COL_QI + IDX_HEADS * IDX_DIM
COL_DQKV = 4096
COL_Z = COL_DQKV + 3 * DELTA_WIDTH
COL_END = COL_Z + DELTA_WIDTH
PROJ_PAD = COL_END
assert COL_SMALL + 128 <= COL_DQKV and COL_DQKV % DELTA_WIDTH == 0

LANES = 128
TOKEN_BLOCK = 512
ATTN_QBLOCK = 256
DELTA_CHUNK = 128
DELTA_HEADS_PER_STEP = 4
SCORE_PAGES_PER_STEP = 8
ATTEND_PAGES_PER_STEP = 8
PEER_EXPERT_BLOCK = 1024
VMEM_LIMIT = 56 * 1024 * 1024

_NT = (((1,), (1,)), ((), ()))


def _round_up(x, m):
    return (x + m - 1) // m * m


def _cparams(*sem):
    return pltpu.CompilerParams(dimension_semantics=sem, vmem_limit_bytes=VMEM_LIMIT)


def _norm_matmul_kernel(x_ref, g_ref, w_ref, o_ref, xn_ref):
    @pl.when(pl.program_id(1) == 0)
    def _():
        x = x_ref[...]
        ms = jnp.mean(x * x, axis=-1, keepdims=True)
        xn_ref[...] = (x * lax.rsqrt(ms + EPS) * g_ref[...]).astype(xn_ref.dtype)

    o_ref[...] = jnp.dot(xn_ref[...], w_ref[...], preferred_element_type=F32).astype(o_ref.dtype)


def norm_matmul(x, gain, w, *, tm, tn, out_dtype, return_normed):
    n, d = x.shape
    width = w.shape[1]
    grid = (n // tm, width // tn)
    in_specs = [
        pl.BlockSpec((tm, d), lambda i, j: (i, 0)),
        pl.BlockSpec((1, d), lambda i, j: (0, 0)),
        pl.BlockSpec((d, tn), lambda i, j: (0, j)),
    ]
    o_spec = pl.BlockSpec((tm, tn), lambda i, j: (i, j))
    o_shape = jax.ShapeDtypeStruct((n, width), out_dtype)
    xn_spec = pl.BlockSpec((tm, d), lambda i, j: (i, 0))
    if return_normed:
        return pl.pallas_call(
            _norm_matmul_kernel, grid=grid, in_specs=in_specs,
            out_specs=[o_spec, xn_spec],
            out_shape=[o_shape, jax.ShapeDtypeStruct((n, d), MXU_DTYPE)],
            compiler_params=_cparams("parallel", "arbitrary"), name="norm_matmul_xn",
        )(x, gain.reshape(1, d).astype(F32), w)
    return pl.pallas_call(
        _norm_matmul_kernel, grid=grid, in_specs=in_specs, out_specs=o_spec, out_shape=o_shape,
        scratch_shapes=[pltpu.VMEM((tm, d), MXU_DTYPE)],
        compiler_params=_cparams("parallel", "arbitrary"), name="norm_matmul",
    )(x, gain.reshape(1, d).astype(F32), w)


def _matmul2_kernel(a1_ref, a2_ref, w1_ref, w2_ref, r_ref, o_ref):
    acc = jnp.dot(a1_ref[...], w1_ref[...], preferred_element_type=F32)
    acc = acc + jnp.dot(a2_ref[...], w2_ref[...], preferred_element_type=F32)
    o_ref[...] = acc + r_ref[...]


def matmul2_residual(a1, a2, w1, w2, r, *, tm, tn):
    n, k1 = a1.shape
    k2 = a2.shape[1]
    d = w1.shape[1]
    return pl.pallas_call(
        _matmul2_kernel, grid=(n // tm, d // tn),
        in_specs=[
            pl.BlockSpec((tm, k1), lambda i, j: (i, 0)),
            pl.BlockSpec((tm, k2), lambda i, j: (i, 0)),
            pl.BlockSpec((k1, tn), lambda i, j: (0, j)),
            pl.BlockSpec((k2, tn), lambda i, j: (0, j)),
            pl.BlockSpec((tm, tn), lambda i, j: (i, j)),
        ],
        out_specs=pl.BlockSpec((tm, tn), lambda i, j: (i, j)),
        out_shape=jax.ShapeDtypeStruct((n, d), F32),
        compiler_params=_cparams("parallel", "arbitrary"), name="out_proj",
    )(a1, a2, w1, w2, r)


def _order_key(s):
    s = jnp.where(s == 0.0, 0.0, s)
    key = lax.bitcast_convert_type(s, jnp.int32)
    return jnp.where(key < 0, key ^ jnp.int32(0x7FFFFFFF), key)


def _count_lanes(mask):
    return jnp.sum(jnp.where(mask, 1.0, 0.0), axis=-1, keepdims=True)


def _count_pages_lanes(mask):
    per_query = jnp.sum(jnp.where(mask, 1.0, 0.0), axis=0, keepdims=True)
    return jnp.sum(per_query, axis=2, keepdims=True)


def _topk_threshold(key, col, topk, idx_bits, count):
    kf = float(topk)
    nonneg = count(key >= 0) >= kf
    base = jnp.where(nonneg, jnp.int32(0), jnp.int32(-2 ** 31))

    def value_bit(it, base):
        cand = base | jnp.left_shift(jnp.int32(1), 30 - it)
        return jnp.where(count(key >= cand) >= kf, cand, base)

    thr = lax.fori_loop(0, 31, value_bit, base)
    tied = key == thr
    need = kf - count(key > thr)

    def index_bit(it, j):
        cand = j | jnp.left_shift(jnp.int32(1), idx_bits - 1 - it)
        return jnp.where(count(tied & (col < cand)) < need, cand, j)

    j = lax.fori_loop(0, idx_bits, index_bit, jnp.zeros_like(thr))
    return thr, j


def _attn_prompt_kernel(q_ref, k_ref, v_ref, qi_ref, ki_ref, wi_ref, o_ref, *, q_start, topk, idx_bits):
    tq = q_ref.shape[1]
    kl = k_ref.shape[1]
    ki = ki_ref[0]
    wi = wi_ref[0]
    s = jnp.zeros((tq, kl), F32)
    for h in range(IDX_HEADS):
        d = lax.dot_general(qi_ref[0, h], ki, _NT, preferred_element_type=F32)
        s = s + jnp.maximum(d, 0.0) * wi[:, h:h + 1]
    row = q_start + lax.broadcasted_iota(jnp.int32, (tq, kl), 0)
    col = lax.broadcasted_iota(jnp.int32, (tq, kl), 1)
    causal = col <= row
    key = _order_key(jnp.where(causal, s, NEG))
    thr, j = _topk_threshold(key, col, topk, idx_bits, _count_lanes)
    sel = (key > thr) | ((key == thr) & (col <= j))
    bias = jnp.where(sel & causal, 0.0, NEG)
    scale = HEAD_DIM ** -0.5
    for h in range(ATTN_HEADS):
        hs = slice(h * HEAD_DIM, (h + 1) * HEAD_DIM)
        logits = lax.dot_general(q_ref[0, :, hs], k_ref[0, :, hs], _NT, preferred_element_type=F32) * scale + bias
        m = jnp.max(logits, axis=-1, keepdims=True)
        p = jnp.exp(logits - m)
        l = jnp.sum(p, axis=-1, keepdims=True)
        o = jnp.dot(p.astype(v_ref.dtype), v_ref[0, :, hs], preferred_element_type=F32)
        o_ref[0, :, hs] = (o / l).astype(o_ref.dtype)


def attn_prompt(qkv, qi, ki, wi, *, tq, topk):
    _, b, tp, w = qkv.shape
    outs = []
    for i in range(tp // tq):
        kl = (i + 1) * tq
        kern = functools.partial(_attn_prompt_kernel, q_start=i * tq, topk=topk,
                                 idx_bits=max(1, (kl - 1).bit_length()))
        outs.append(pl.pallas_call(
            kern, grid=(b,),
            in_specs=[
                pl.BlockSpec((None, 1, tq, w), lambda bb, i=i: (0, bb, i, 0)),
                pl.BlockSpec((None, 1, kl, w), lambda bb: (1, bb, 0, 0)),
                pl.BlockSpec((None, 1, kl, w), lambda bb: (2, bb, 0, 0)),
                pl.BlockSpec((1, IDX_HEADS, tq, IDX_DIM), lambda bb, i=i: (bb, 0, i, 0)),
                pl.BlockSpec((1, kl, IDX_DIM), lambda bb: (bb, 0, 0)),
                pl.BlockSpec((1, tq, IDX_HEADS), lambda bb, i=i: (bb, i, 0)),
            ],
            out_specs=pl.BlockSpec((1, tq, w), lambda bb: (bb, 0, 0)),
            out_shape=jax.ShapeDtypeStruct((b, tq, w), MXU_DTYPE),
            compiler_params=_cparams("parallel"), name=f"attn_prompt_q{i}",
        )(qkv, qkv, qkv, qi, ki, wi))
    return jnp.concatenate(outs, axis=1)


def _page_scores(qi, wcol, kpage, ds):
    d = lax.dot_general(qi, kpage.astype(qi.dtype), _NT, preferred_element_type=F32)
    r = jnp.maximum(d, 0.0) * wcol
    return jnp.sum(r.reshape(ds, IDX_HEADS, r.shape[-1]), axis=1)


def _sample_scores_kernel(pt_ref, qi_ref, wi_ref, knew_ref, *rest, n_pages_step, ds):
    page_refs = rest[:n_pages_step]
    past_ref, new_ref = rest[n_pages_step:]
    qi = qi_ref[0]
    wcol = wi_ref[0]
    for g in range(n_pages_step):
        past_ref[0, g] = _page_scores(qi, wcol, page_refs[g][0], ds)
    s_new = _page_scores(qi, wcol, knew_ref[0], ds)
    qrow = lax.broadcasted_iota(jnp.int32, s_new.shape, 0)
    kcol = lax.broadcasted_iota(jnp.int32, s_new.shape, 1)
    new_ref[0] = jnp.where(kcol <= qrow, s_new, NEG)


def sample_scores(qi, wi, ki_new_page, cache_kidx, page_table, *, ds):
    db, n_pages = page_table.shape
    g = math.gcd(SCORE_PAGES_PER_STEP, n_pages)
    rows = ds * IDX_HEADS
    page_specs = [pl.BlockSpec((1, PAGE_SIZE, IDX_DIM), lambda b, p, pt, gg=gg: (pt[b, p * g + gg], 0, 0))
                  for gg in range(g)]
    grid_spec = pltpu.PrefetchScalarGridSpec(
        num_scalar_prefetch=1, grid=(db, n_pages // g),
        in_specs=[pl.BlockSpec((1, rows, IDX_DIM), lambda b, p, pt: (b, 0, 0)),
                  pl.BlockSpec((1, rows, 1), lambda b, p, pt: (b, 0, 0)),
                  pl.BlockSpec((1, PAGE_SIZE, IDX_DIM), lambda b, p, pt: (b, 0, 0))] + page_specs,
        out_specs=[pl.BlockSpec((1, g, ds, PAGE_SIZE), lambda b, p, pt: (b, p, 0, 0)),
                   pl.BlockSpec((1, ds, PAGE_SIZE), lambda b, p, pt: (b, 0, 0))])
    return pl.pallas_call(
        functools.partial(_sample_scores_kernel, n_pages_step=g, ds=ds), grid_spec=grid_spec,
        out_shape=[jax.ShapeDtypeStruct((db, n_pages, ds, PAGE_SIZE), F32),
                   jax.ShapeDtypeStruct((db, ds, PAGE_SIZE), F32)],
        compiler_params=_cparams("parallel", "arbitrary"), name="sample_scores",
    )(page_table, qi, wi, ki_new_page, *([cache_kidx] * g))


def _sample_attend_kernel(pt_ref, sc_ref, scn_ref, q_ref, kn_ref, vn_ref, *rest, n_pages_step, ds, topk, idx_bits):
    k_refs = rest[:n_pages_step]
    v_refs = rest[n_pages_step:2 * n_pages_step]
    o_ref, thr_ref, j_ref, m_ref, l_ref, acc_ref = rest[2 * n_pages_step:]
    p = pl.program_id(1)
    n_pages = sc_ref.shape[1]
    nh = ATTN_HEADS
    scale = HEAD_DIM ** -0.5
    group_w = ds * nh
    pack = LANES // group_w
    page_rows = PAGE_SIZE * nh
    sub = lax.broadcasted_iota(jnp.int32, (nh, LANES), 0)
    lane = lax.broadcasted_iota(jnp.int32, (nh, LANES), 1)
    head_bias = jnp.where(lane % nh == sub, 0.0, NEG)
    qrow = lax.broadcasted_iota(jnp.int32, (LANES, LANES), 0)
    qlane = lax.broadcasted_iota(jnp.int32, (LANES, LANES), 1)
    expand = jnp.where(qlane // nh == qrow, 1.0, 0.0).astype(q_ref.dtype)
    slot_of_lane = lax.broadcasted_iota(jnp.int32, (1, LANES), 1) // group_w

    @pl.when(p == 0)
    def _():
        key = _order_key(jnp.concatenate([sc_ref[0], scn_ref[0][None]], axis=0))
        col = (lax.broadcasted_iota(jnp.int32, key.shape, 0) * PAGE_SIZE
               + lax.broadcasted_iota(jnp.int32, key.shape, 2))
        thr, j = _topk_threshold(key, col, topk, idx_bits, _count_pages_lanes)
        thr_ref[...] = thr[0]
        j_ref[...] = j[0]
        m_ref[...] = jnp.full_like(m_ref, NEG)
        l_ref[...] = jnp.zeros_like(l_ref)
        acc_ref[...] = jnp.zeros_like(acc_ref)

    def selected(scores, page_idx):
        key = _order_key(scores)
        col = page_idx * PAGE_SIZE + lax.broadcasted_iota(jnp.int32, key.shape, 1)
        sel = ((key > thr_ref[...]) | ((key == thr_ref[...]) & (col <= j_ref[...]))) & (scores > 0.5 * NEG)
        return jnp.where(sel, 1.0, 0.0)

    def attend(sels, kpages, vpages):
        empty = jnp.zeros((ds, PAGE_SIZE), F32)
        sel_rows = jnp.concatenate([empty if s is None else s for s in sels]
                                   + [jnp.zeros((LANES - pack * ds, PAGE_SIZE), F32)], axis=0)
        spread = jnp.dot(sel_rows.T.astype(q_ref.dtype), expand, preferred_element_type=F32)
        key_bias = jnp.where(spread > 0.5, 0.0, NEG)
        bias = (key_bias[:, None, :] + head_bias[None, :, :]).reshape(page_rows, LANES)
        kcat = jnp.concatenate([kp.astype(q_ref.dtype) for kp in kpages], axis=1)
        logits = jnp.dot(kcat, q_ref[0], preferred_element_type=F32) * scale + bias
        m_new = jnp.maximum(m_ref[...], jnp.max(logits, axis=0, keepdims=True))
        alpha = jnp.exp(m_ref[...] - m_new)
        pr = jnp.exp(logits - m_new)
        l_ref[...] = alpha * l_ref[...] + jnp.sum(pr, axis=0, keepdims=True)
        vt = jnp.concatenate([vp.T.astype(q_ref.dtype) for vp in vpages], axis=0)
        pv = jnp.dot(vt, pr.astype(q_ref.dtype), preferred_element_type=F32)
        new = jnp.zeros((HEAD_DIM, LANES), F32)
        for s in range(pack):
            new = new + jnp.where(slot_of_lane == s, pv[s * HEAD_DIM:(s + 1) * HEAD_DIM], 0.0)
        acc_ref[...] = alpha * acc_ref[...] + new
        m_ref[...] = m_new

    for tile in range(n_pages_step // pack):
        idx = [p * n_pages_step + tile * pack + s for s in range(pack)]
        attend([selected(sc_ref[0, i], i) for i in idx],
               [k_refs[tile * pack + s][0] for s in range(pack)], [v_refs[tile * pack + s][0] for s in range(pack)])

    @pl.when(p == pl.num_programs(1) - 1)
    def _():
        nothing = jnp.zeros((page_rows, HEAD_DIM), F32)
        attend([selected(scn_ref[0], n_pages)] + [None] * (pack - 1),
               [kn_ref[0]] + [nothing] * (pack - 1), [vn_ref[0]] + [nothing] * (pack - 1))
        m = m_ref[...]
        m_all = m
        for s in range(1, pack):
            m_all = jnp.maximum(m_all, pltpu.roll(m, s * group_w, 1))
        w = jnp.exp(m - m_all)
        l_w = l_ref[...] * w
        a_w = acc_ref[...] * w
        l_all, a_all = l_w, a_w
        for s in range(1, pack):
            l_all = l_all + pltpu.roll(l_w, s * group_w, 1)
            a_all = a_all + pltpu.roll(a_w, s * group_w, 1)
        o_ref[0] = a_all / l_all


def sample_attend(scores, scores_new, q_cols, k_new_page, v_new_page, cache_k, cache_v, page_table, *, ds, topk):
    db, n_pages = page_table.shape
    g = math.gcd(ATTEND_PAGES_PER_STEP, n_pages)
    pack = q_cols.shape[1] // HEAD_DIM
    assert g % pack == 0 and pack * ds * ATTN_HEADS == LANES
    page_rows = PAGE_SIZE * ATTN_HEADS
    idx_bits = max(1, ((n_pages + 1) * PAGE_SIZE - 1).bit_length())

    def page_spec(gg):
        return pl.BlockSpec((1, page_rows, HEAD_DIM), lambda b, p, pt, gg=gg: (pt[b, p * g + gg], 0, 0))

    grid_spec = pltpu.PrefetchScalarGridSpec(
        num_scalar_prefetch=1, grid=(db, n_pages // g),
        in_specs=[pl.BlockSpec((1, n_pages, ds, PAGE_SIZE), lambda b, p, pt: (b, 0, 0, 0)),
                  pl.BlockSpec((1, ds, PAGE_SIZE), lambda b, p, pt: (b, 0, 0)),
                  pl.BlockSpec((1, pack * HEAD_DIM, LANES), lambda b, p, pt: (b, 0, 0)),
                  pl.BlockSpec((1, page_rows, HEAD_DIM), lambda b, p, pt: (b, 0, 0)),
                  pl.BlockSpec((1, page_rows, HEAD_DIM), lambda b, p, pt: (b, 0, 0))]
        + [page_spec(gg) for gg in range(g)] * 2,
        out_specs=pl.BlockSpec((1, HEAD_DIM, LANES), lambda b, p, pt: (b, 0, 0)),
        scratch_shapes=[pltpu.VMEM((ds, 1), jnp.int32), pltpu.VMEM((ds, 1), jnp.int32),
                        pltpu.VMEM((1, LANES), F32), pltpu.VMEM((1, LANES), F32), pltpu.VMEM((HEAD_DIM, LANES), F32)])
    kern = functools.partial(_sample_attend_kernel, n_pages_step=g, ds=ds, topk=topk, idx_bits=idx_bits)
    return pl.pallas_call(
        kern, grid_spec=grid_spec, out_shape=jax.ShapeDtypeStruct((db, HEAD_DIM, LANES), F32),
        compiler_params=_cparams("parallel", "arbitrary"), name="sample_attend",
    )(page_table, scores, scores_new, q_cols, k_new_page, v_new_page, *([cache_k] * g), *([cache_v] * g))


def _split_hi_lo(x):
    hi = x.astype(MXU_DTYPE)
    return hi, (x - hi.astype(F32)).astype(MXU_DTYPE)


def _dot1(a, b, dims=None):
    if dims is None:
        dims = (((1,), (0,)), ((), ()))
    return lax.dot_general(a.astype(MXU_DTYPE), b.astype(MXU_DTYPE), dims, preferred_element_type=F32)


def _dot3(a, b, dims=None):
    ah, al = _split_hi_lo(a)
    bh, bl = _split_hi_lo(b)
    if dims is None:
        dims = (((1,), (0,)), ((), ()))
    d = functools.partial(lax.dot_general, dimension_numbers=dims, preferred_element_type=F32)
    return d(ah, bh) + (d(ah, bl) + d(al, bh))


def _delta_kernel(q_ref, k_ref, v_ref, z_ref, gcum_ref, beta_ref, s0_ref, dn_ref, o_ref, s_ref, t_scr, qk_scr,
                  *, n_chunks):
    c_len = DELTA_CHUNK
    n_h = s0_ref.shape[1]
    ri = lax.broadcasted_iota(jnp.int32, (c_len, c_len), 0)
    ci = lax.broadcasted_iota(jnp.int32, (c_len, c_len), 1)
    incl = ri >= ci
    strict = ri > ci
    eye = jnp.where(ri == ci, 1.0, 0.0)
    dn = dn_ref[...]

    def chunk_rows(c):
        return pl.ds(pl.multiple_of(c * c_len, c_len), c_len)

    def decay_terms(c, hh):
        g_row = jnp.broadcast_to(gcum_ref[0, hh, pl.ds(c, 1), :], (c_len, c_len))
        b_col = jnp.broadcast_to(beta_ref[0, hh, pl.ds(c, 1), :], (c_len, c_len)).T
        return g_row, g_row.T, b_col

    heads = range(n_h)
    head_cols = [slice(hh * DELTA_DK, (hh + 1) * DELTA_DK) for hh in heads]

    def build(c, carry):
        rows = chunk_rows(c)
        qc = [q_ref[0, rows, cs] for cs in head_cols]
        kc = [k_ref[0, rows, cs] for cs in head_cols]
        terms = [decay_terms(c, hh) for hh in heads]
        dec = [jnp.where(incl, jnp.exp(jnp.where(incl, g_col - g_row, 0.0)), 0.0) for g_row, g_col, _ in terms]
        kk = [_dot3(kc[hh], kc[hh], _NT) for hh in heads]
        a = [jnp.where(strict, kk[hh] * dec[hh], 0.0) * terms[hh][2] for hh in heads]
        t = [eye - a[hh] for hh in heads]
        pw = [_dot1(a[hh], a[hh]) for hh in heads]
        span = 2
        while span < c_len:
            t = [t[hh] + _dot1(t[hh], pw[hh]) for hh in heads]
            span *= 2
            if span < c_len:
                pw = [_dot1(pw[hh], pw[hh]) for hh in heads]
        at = [_dot3(a[hh], t[hh]) for hh in heads]
        corr = [_dot3(t[hh], eye - t[hh] - at[hh]) for hh in heads]
        qk = [_dot1(qc[hh], kc[hh], _NT) for hh in heads]
        for hh in heads:
            t_scr[hh, c] = t[hh] + corr[hh]
            qk_scr[hh, c] = qk[hh] * dec[hh]
        return carry

    lax.fori_loop(0, n_chunks, build, 0)

    def step(c, states):
        rows = chunk_rows(c)
        qc = [q_ref[0, rows, cs] for cs in head_cols]
        kc = [k_ref[0, rows, cs] for cs in head_cols]
        terms = [decay_terms(c, hh) for hh in heads]
        e_g = [jnp.exp(g_col) for _, g_col, _ in terms]
        ks = [_dot1(kc[hh], states[hh]) for hh in heads]
        qs = [_dot1(qc[hh], states[hh]) for hh in heads]
        u = [_dot3(t_scr[hh, c], terms[hh][2] * (v_ref[0, rows, head_cols[hh]] - e_g[hh] * ks[hh])) for hh in heads]
        qku = [_dot1(qk_scr[hh, c], u[hh]) for hh in heads]
        g_last = [g_row[:, c_len - 1:c_len] for g_row, _, _ in terms]
        kw = [kc[hh] * jnp.exp(g_last[hh] - terms[hh][1]) for hh in heads]
        upd = [_dot1(kw[hh].T, u[hh]) for hh in heads]
        for hh in heads:
            o = e_g[hh] * qs[hh] + qku[hh]
            z = z_ref[0, rows, head_cols[hh]]
            on = o * lax.rsqrt(jnp.mean(o * o, axis=-1, keepdims=True) + EPS) * dn
            o_ref[0, rows, head_cols[hh]] = (on * (z / (1.0 + jnp.exp(-z)))).astype(o_ref.dtype)
        return tuple(jnp.exp(g_last[hh]) * states[hh] + upd[hh] for hh in heads)

    final = lax.fori_loop(0, n_chunks, step, tuple(s0_ref[0, hh] for hh in range(n_h)))
    for hh in range(n_h):
        s_ref[0, hh] = final[hh]


def gated_delta(qkvz, gcum, beta, s0, delta_norm, *, n_chunks):
    _, b, tp, _ = qkvz.shape
    h = DELTA_HEADS
    n_h = DELTA_HEADS_PER_STEP
    tok = pl.BlockSpec((1, tp, n_h * DELTA_DV), lambda bb, hh: (bb, 0, hh))
    ops = [pl.BlockSpec((None, 1, tp, n_h * DELTA_DV), lambda bb, hh, g=g: (g, bb, 0, hh)) for g in range(4)]
    row = pl.BlockSpec((1, n_h, n_chunks, DELTA_CHUNK), lambda bb, hh: (bb, hh, 0, 0))
    st = pl.BlockSpec((1, n_h, DELTA_DK, DELTA_DV), lambda bb, hh: (bb, hh, 0, 0))
    mat = pltpu.VMEM((n_h, n_chunks, DELTA_CHUNK, DELTA_CHUNK), F32)
    return pl.pallas_call(
        functools.partial(_delta_kernel, n_chunks=n_chunks), grid=(b, h // n_h),
        in_specs=ops + [row, row, st, pl.BlockSpec((1, DELTA_DV), lambda bb, hh: (0, 0))],
        out_specs=[tok, st],
        out_shape=[jax.ShapeDtypeStruct((b, tp, h * DELTA_DV), MXU_DTYPE),
                   jax.ShapeDtypeStruct((b, h, DELTA_DK, DELTA_DV), F32)],
        scratch_shapes=[mat, mat],
        compiler_params=_cparams("parallel", "parallel"), name="gated_delta",
    )(qkvz, qkvz, qkvz, qkvz, gcum, beta, s0, delta_norm.reshape(1, DELTA_DV).astype(F32))


def _top_values(cur, n):
    vals = []
    for r in range(n):
        m = jnp.max(cur, axis=0, keepdims=True)
        vals.append(m)
        if r + 1 < n:
            cur = jnp.where(cur == m, -jnp.inf, cur)
    return vals


_PEER_RANKS = PEER_TOPK + 1
_PEER_PAIRS = [(r, c) for r in range(_PEER_RANKS) for c in range(_PEER_RANKS) if (r + 1) * (c + 1) <= _PEER_RANKS]


def _peer_select_kernel(q_ref, sk_ref, a_ref, b_ref, misc_ref):
    tn = q_ref.shape[1]

    def head(h, carry):
        shifted = []
        tops = []
        for c in range(2):
            st = lax.dot_general(sk_ref[c], q_ref[h * 2 + c], _NT, preferred_element_type=F32)
            vals = _top_values(st, _PEER_RANKS)
            shifted.append(st - vals[0])
            tops.append([v - vals[0] for v in vals])
        a_ref[h] = shifted[0]
        b_ref[h] = shifted[1]
        cand = [tops[0][r] + tops[1][c] for r, c in _PEER_PAIRS]
        pad = _round_up(len(cand), 8) - len(cand)
        cand = jnp.concatenate(cand + [jnp.full((pad, tn), -jnp.inf, F32)], axis=0)
        best = _top_values(cand, _PEER_RANKS)
        zsum = jnp.zeros_like(best[0])
        for v in best[:PEER_TOPK]:
            zsum = zsum + jnp.exp(v)
        thr = 0.5 * (best[PEER_TOPK - 1] + best[PEER_TOPK])
        misc_ref[h] = jnp.concatenate([thr, 1.0 / zsum, jnp.zeros((6, tn), F32)], axis=0)
        return carry

    lax.fori_loop(0, PEER_HEADS, head, 0)


def peer_select(q, subkeys, *, tn):
    n = q.shape[1]
    blk = pl.BlockSpec((PEER_HEADS, PEER_NKEYS, tn), lambda i: (0, 0, i))
    return pl.pallas_call(
        _peer_select_kernel, grid=(n // tn,),
        in_specs=[pl.BlockSpec((q.shape[0], tn, q.shape[2]), lambda i: (0, i, 0)),
                  pl.BlockSpec(subkeys.shape, lambda i: (0, 0, 0))],
        out_specs=[blk, blk, pl.BlockSpec((PEER_HEADS, 8, tn), lambda i: (0, 0, i))],
        out_shape=[jax.ShapeDtypeStruct((PEER_HEADS, PEER_NKEYS, n), F32),
                   jax.ShapeDtypeStruct((PEER_HEADS, PEER_NKEYS, n), F32),
                   jax.ShapeDtypeStruct((PEER_HEADS, 8, n), F32)],
        compiler_params=_cparams("parallel"), name="peer_select",
    )(q, subkeys)


def _gelu_tanh(x):
    return 0.5 * x * (1.0 + jnp.tanh(math.sqrt(2.0 / math.pi) * (x + 0.044715 * (x * x * x))))


def _peer_dense_kernel(xn_ref, a_ref, b_ref, misc_ref, u_ref, vt_ref, x_ref, gf_ref, o_ref,
                       acc_ref, bexp_ref, g_ref, row_ref, ht_ref):
    e = pl.program_id(1)
    te = u_ref.shape[0]
    tn = xn_ref.shape[0]
    nk = PEER_NKEYS

    @pl.when(e == 0)
    def _():
        acc_ref[...] = jnp.zeros_like(acc_ref)
        for h in range(PEER_HEADS):
            bexp_ref[h] = jnp.exp(b_ref[h])

    n_i = te // nk
    col_w = min(tn, 2 * LANES)
    n_col = tn // col_w

    def up_piece(c):
        cs = slice(c * col_w, (c + 1) * col_w)
        ht_ref[:, cs] = lax.dot_general(u_ref[...], xn_ref[cs, :], _NT, preferred_element_type=F32)

    def down_piece(c):
        cs = slice(c * col_w, (c + 1) * col_w)
        acc_ref[:, cs] += jnp.dot(vt_ref[...], g_ref[:, cs], preferred_element_type=F32)

    thr_all = jnp.concatenate([misc_ref[h, 0:1, :] for h in range(PEER_HEADS)], axis=0)
    invz_all = jnp.concatenate([misc_ref[h, 1:2, :] for h in range(PEER_HEADS)], axis=0)
    for ii in range(n_i):
        a_rows = jnp.concatenate([a_ref[h, pl.ds(e * n_i + ii, 1), :] for h in range(PEER_HEADS)], axis=0)
        row_ref[0, ii] = thr_all - a_rows
        row_ref[1, ii] = jnp.exp(a_rows) * invz_all

    key_tile = 16

    def gate_columns(c):
        for tb in range(c * col_w // LANES, (c + 1) * col_w // LANES):
            lanes = slice(tb * LANES, (tb + 1) * LANES)
            for kt in range(nk // key_tile):
                keys = slice(kt * key_tile, (kt + 1) * key_tile)
                w = [None] * n_i
                for h in range(PEER_HEADS):
                    b_tile = b_ref[h, keys, lanes]
                    bexp_tile = bexp_ref[h, keys, lanes]
                    for ii in range(n_i):
                        wh = (jnp.where(b_tile >= row_ref[0, ii, h:h + 1, lanes], bexp_tile, 0.0)
                              * row_ref[1, ii, h:h + 1, lanes])
                        w[ii] = wh if w[ii] is None else w[ii] + wh
                for ii in range(n_i):
                    rows = slice(ii * nk + kt * key_tile, ii * nk + (kt + 1) * key_tile)
                    g_ref[rows, lanes] = (w[ii] * _gelu_tanh(ht_ref[rows, lanes])).astype(g_ref.dtype)

    up_piece(0)
    for c in range(n_col):
        if c + 1 < n_col:
            up_piece(c + 1)
        gate_columns(c)
        down_piece(c)

    @pl.when(e == pl.num_programs(1) - 1)
    def _():
        y = acc_ref[...].T + x_ref[...]
        ms = jnp.mean(y * y, axis=-1, keepdims=True)
        o_ref[...] = y * lax.rsqrt(ms + EPS) * gf_ref[...]


def peer_dense(xn, a, b, misc, u, vt, x, gain_final, *, tn, te):
    n, d = x.shape
    n_blocks = u.shape[0] // te
    once = pl.Buffered(1)
    sel = pl.BlockSpec((PEER_HEADS, PEER_NKEYS, tn), lambda i, e: (0, 0, i), pipeline_mode=once)
    return pl.pallas_call(
        _peer_dense_kernel, grid=(n // tn, n_blocks),
        in_specs=[
            pl.BlockSpec((tn, d), lambda i, e: (i, 0), pipeline_mode=once),
            sel, sel,
            pl.BlockSpec((PEER_HEADS, 8, tn), lambda i, e: (0, 0, i)),
            pl.BlockSpec((te, d), lambda i, e: (e, 0)),
            pl.BlockSpec((d, te), lambda i, e: (0, e)),
            pl.BlockSpec((tn, d), lambda i, e: (i, 0), pipeline_mode=once),
            pl.BlockSpec((1, d), lambda i, e: (0, 0)),
        ],
        out_specs=pl.BlockSpec((tn, d), lambda i, e: (i, 0)),
        out_shape=jax.ShapeDtypeStruct((n, d), F32),
        scratch_shapes=[pltpu.VMEM((d, tn), F32),
                        pltpu.VMEM((PEER_HEADS, PEER_NKEYS, tn), F32),
                        pltpu.VMEM((te, tn), MXU_DTYPE),
                        pltpu.VMEM((2, te // PEER_NKEYS, PEER_HEADS, tn), F32),
                        pltpu.VMEM((te, tn), F32)],
        compiler_params=_cparams("parallel", "arbitrary"), name="peer_dense",
    )(xn, a, b, misc, u, vt, x, gain_final.reshape(1, d).astype(F32))


PREP_LANES = 256


def _rope_cast_kernel(x_ref, c_ref, s1_ref, s2_ref, obf_ref, kf_ref, vf_ref):
    g = pl.program_id(2)
    t = x_ref.shape[1]
    tp = obf_ref.shape[2]
    x = x_ref[0]

    def emit(y):
        obf_ref[0, 0, :t, :] = y.astype(obf_ref.dtype)
        if tp > t:
            obf_ref[0, 0, t:, :] = jnp.zeros((tp - t, y.shape[1]), obf_ref.dtype)

    def rotary():
        parts = []
        for h in range(x.shape[1] // HEAD_DIM):
            xs = x[:, h * HEAD_DIM:(h + 1) * HEAD_DIM]
            half = HEAD_DIM // 8
            parts.append(xs * c_ref[...] + pltpu.roll(xs, half, 1) * s1_ref[...]
                         + pltpu.roll(xs, HEAD_DIM - half, 1) * s2_ref[...])
        return jnp.concatenate(parts, axis=1)

    @pl.when(g == 0)
    def _():
        emit(rotary())

    @pl.when(g == 1)
    def _():
        y = rotary()
        kf_ref[0] = y
        emit(y)

    @pl.when(g == 2)
    def _():
        vf_ref[0] = x
        emit(x)


def rope_cast(p, pos, *, tp):
    b, t, _ = p.shape
    rot = HEAD_DIM // 4
    half = rot // 2
    inv = ROPE_THETA ** (-jnp.arange(half, dtype=F32) * 2.0 / rot)
    ang = pos.astype(F32)[:, None] * inv[None, :]
    cos, sin, zero = jnp.cos(ang), jnp.sin(ang), jnp.zeros((t, half), F32)
    rest = HEAD_DIM - rot
    c = jnp.concatenate([cos, cos, jnp.ones((t, rest), F32)], axis=1)
    s1 = jnp.concatenate([zero, sin, jnp.zeros((t, rest), F32)], axis=1)
    s2 = jnp.concatenate([-sin, zero, jnp.zeros((t, rest), F32)], axis=1)
    per_group = ATTN_WIDTH // PREP_LANES
    tab = pl.BlockSpec((t, HEAD_DIM), lambda bb, j, g: (0, 0))
    f32_out = pl.BlockSpec((1, t, PREP_LANES), lambda bb, j, g: (bb, 0, j))
    return pl.pallas_call(
        _rope_cast_kernel, grid=(b, per_group, 3),
        in_specs=[pl.BlockSpec((1, t, PREP_LANES), lambda bb, j, g: (bb, 0, g * per_group + j)), tab, tab, tab],
        out_specs=[pl.BlockSpec((1, 1, tp, PREP_LANES), lambda bb, j, g: (g, bb, 0, j)), f32_out, f32_out],
        out_shape=[jax.ShapeDtypeStruct((3, b, tp, ATTN_WIDTH), MXU_DTYPE),
                   jax.ShapeDtypeStruct((b, t, ATTN_WIDTH), F32), jax.ShapeDtypeStruct((b, t, ATTN_WIDTH), F32)],
        compiler_params=_cparams("parallel", "parallel", "arbitrary"), name="rope_cast",
    )(p, c, s1, s2)


def _delta_prep_kernel(x_ref, prev_ref, w_ref, o_ref):
    g = pl.program_id(1)
    t = x_ref.shape[1]
    tp = o_ref.shape[2]
    x = x_ref[0]

    def emit(y):
        o_ref[0, 0, :t, :] = y
        if tp > t:
            o_ref[0, 0, t:, :] = jnp.zeros((tp - t, y.shape[1]), F32)

    def conv_silu():
        w = w_ref[...]
        head = jnp.concatenate([prev_ref[0], x[:8]], axis=0)
        acc = x * w[CONV_WIDTH - 1:CONV_WIDTH]
        for s in range(1, CONV_WIDTH):
            shifted = jnp.concatenate([head[8 - s:16 - s], pltpu.roll(x, s, 0)[8:]], axis=0)
            acc = acc + shifted * w[CONV_WIDTH - 1 - s:CONV_WIDTH - s]
        return acc * (1.0 / (1.0 + jnp.exp(-acc)))

    @pl.when(g < 2)
    def _():
        y = conv_silu()
        scale = jnp.where(g == 0, DELTA_DK ** -0.5, 1.0)
        parts = []
        for h in range(y.shape[1] // DELTA_DK):
            ys = y[:, h * DELTA_DK:(h + 1) * DELTA_DK]
            parts.append(ys * lax.rsqrt(jnp.sum(ys * ys, axis=-1, keepdims=True) + EPS) * scale)
        emit(jnp.concatenate(parts, axis=1))

    @pl.when(g == 2)
    def _():
        emit(conv_silu())

    @pl.when(g == 3)
    def _():
        emit(x)


def delta_prep(p, conv_prev, conv_w, *, tp):
    b, t, _ = p.shape
    assert t >= 8
    per_group = DELTA_WIDTH // PREP_LANES
    first = COL_DQKV // PREP_LANES
    prev8 = jnp.pad(conv_prev.astype(F32), ((0, 0), (8 - (CONV_WIDTH - 1), 0), (0, 0)))

    def conv_col(g, j):
        return jnp.minimum(g, 2) * per_group + j

    return pl.pallas_call(
        _delta_prep_kernel, grid=(b, 4, per_group),
        in_specs=[pl.BlockSpec((1, t, PREP_LANES), lambda bb, g, j: (bb, 0, first + g * per_group + j)),
                  pl.BlockSpec((1, 8, PREP_LANES), lambda bb, g, j: (bb, 0, conv_col(g, j))),
                  pl.BlockSpec((CONV_WIDTH, PREP_LANES), lambda bb, g, j: (0, conv_col(g, j)))],
        out_specs=pl.BlockSpec((1, 1, tp, PREP_LANES), lambda bb, g, j: (g, bb, 0, j)),
        out_shape=jax.ShapeDtypeStruct((4, b, tp, DELTA_WIDTH), F32),
        compiler_params=_cparams("parallel", "parallel", "parallel"), name="delta_prep",
    )(p, prev8, conv_w.astype(F32))


def _rope(x, pos):
    rot = x.shape[-1] // 4
    half = rot // 2
    inv = ROPE_THETA ** (-jnp.arange(half, dtype=F32) * 2.0 / rot)
    ang = pos.astype(F32)[:, None] * inv[None, :]
    cos = jnp.cos(ang)[None, :, None, :]
    sin = jnp.sin(ang)[None, :, None, :]
    x1 = x[..., :half]
    x2 = x[..., half:rot]
    return jnp.concatenate([x1 * cos - x2 * sin, x2 * cos + x1 * sin, x[..., rot:]], axis=-1)


def _l2norm(x):
    return x * lax.rsqrt(jnp.sum(x * x, axis=-1, keepdims=True) + EPS)


def _small_projection(p, pos, a_log, dt_bias):
    b, t, _ = p.shape
    qi = _rope(p[..., COL_QI:COL_SMALL].reshape(b, t, IDX_HEADS, IDX_DIM), pos)
    ki = _rope(p[..., COL_SMALL:COL_SMALL + IDX_DIM][:, :, None, :], pos)[:, :, 0, :]
    o = COL_SMALL + IDX_DIM
    wi = p[..., o:o + IDX_HEADS] * (IDX_HEADS * IDX_DIM) ** -0.5
    beta = jax.nn.sigmoid(p[..., o + IDX_HEADS:o + IDX_HEADS + DELTA_HEADS])
    a = p[..., o + IDX_HEADS + DELTA_HEADS:o + IDX_HEADS + 2 * DELTA_HEADS]
    g = -jnp.exp(a_log.astype(F32)) * jax.nn.softplus(a + dt_bias.astype(F32))
    return qi, ki, wi, g, beta


def _split_projection(p, pos, conv_prev, conv_w, a_log, dt_bias):
    b, t, _ = p.shape
    q = _rope(p[..., COL_Q:COL_K].reshape(b, t, ATTN_HEADS, HEAD_DIM), pos)
    k = _rope(p[..., COL_K:COL_V].reshape(b, t, ATTN_HEADS, HEAD_DIM), pos)
    v = p[..., COL_V:COL_QI].reshape(b, t, ATTN_HEADS, HEAD_DIM)
    qi, ki, wi, g, beta = _small_projection(p, pos, a_log, dt_bias)
    dqkv = p[..., COL_DQKV:COL_Z]
    z = p[..., COL_Z:COL_END]
    xpad = jnp.concatenate([conv_prev.astype(F32), dqkv], axis=1)
    conv = sum(conv_w[i] * xpad[:, i:i + t] for i in range(CONV_WIDTH))
    conv = jax.nn.silu(conv)
    new_conv = xpad[:, t:]
    dq = _l2norm(conv[..., :DELTA_WIDTH].reshape(b, t, DELTA_HEADS, DELTA_DK)) * DELTA_DK ** -0.5
    dk = _l2norm(conv[..., DELTA_WIDTH:2 * DELTA_WIDTH].reshape(b, t, DELTA_HEADS, DELTA_DK))
    dv = conv[..., 2 * DELTA_WIDTH:]
    return (q, k, v, qi, ki, wi), (dq.reshape(b, t, DELTA_WIDTH), dk.reshape(b, t, DELTA_WIDTH), dv, g, beta, z), new_conv


def _pad_rows(x, tp):
    return jnp.pad(x, [(0, 0), (0, tp - x.shape[1])] + [(0, 0)] * (x.ndim - 2))


def _delta_rows(t):
    n_chunks = -(-t // DELTA_CHUNK)
    return n_chunks, n_chunks * DELTA_CHUNK


def _delta_group(qkvz, g, beta, s0, delta_norm):
    b, t, _ = g.shape
    n_chunks, tp = _delta_rows(t)

    def rows(x):
        return _pad_rows(x, tp).reshape(b, n_chunks, DELTA_CHUNK, DELTA_HEADS).transpose(0, 3, 1, 2)

    gcum = jnp.cumsum(rows(g), axis=-1)
    o, s = gated_delta(qkvz, gcum, rows(beta), s0.astype(F32), delta_norm, n_chunks=n_chunks)
    return o[:, :t], s


def _sample_attention(q, k, v, qi, ki, wi, cache_k, cache_v, cache_kidx, page_table, topk):
    db, ds, h, hd = q.shape
    n_pool = cache_k.shape[0]

    def new_page(x):
        return _pad_rows(x.reshape(db, ds, -1), PAGE_SIZE)

    scores, scores_new = sample_scores(
        qi.reshape(db, ds * IDX_HEADS, IDX_DIM).astype(MXU_DTYPE), wi.reshape(db, ds * IDX_HEADS, 1),
        new_page(ki), cache_kidx, page_table, ds=ds)
    pack = LANES // (ds * h)
    q_t = q.reshape(db, ds * h, hd).transpose(0, 2, 1)
    q_cols = jnp.concatenate(
        [jnp.pad(q_t, ((0, 0), (0, 0), (s * ds * h, LANES - (s + 1) * ds * h))) for s in range(pack)], axis=1)

    def key_head_rows(x):
        return x.reshape(x.shape[:-3] + (x.shape[-3] * h, hd))

    o = sample_attend(scores, scores_new, q_cols.astype(MXU_DTYPE),
                      key_head_rows(_pad_rows(k, PAGE_SIZE)), key_head_rows(_pad_rows(v, PAGE_SIZE)),
                      key_head_rows(cache_k), key_head_rows(cache_v), page_table, ds=ds, topk=topk)
    return o[:, :, :ds * h].transpose(0, 2, 1).reshape(db, ds, h * hd)


def _token_tile(n, want):
    if n % LANES:
        return n
    best = LANES
    for m in range(LANES, want + 1, LANES):
        if n % m == 0:
            best = m
    return best


def _layer(l, x, x_groups, dims, cache_k, cache_v, cache_kidx, state_conv, state_ssm, page_table,
           norm_mix, w_in, conv_w, a_log, dt_bias, delta_norm, w_out, norm_ffn,
           peer_wq, peer_subkeys, peer_u, peer_v, norm_out):
    b, t, db, ds = dims
    n_pad, d = x.shape
    n_p, n_s = b * t, db * ds
    past = page_table.shape[1] * PAGE_SIZE
    tm = _token_tile(n_pad, 1280)

    w = w_in[l]
    o_wi = 3 * ATTN_WIDTH + IDX_HEADS * IDX_DIM + IDX_DIM + IDX_HEADS
    o_z = o_wi + 3 * DELTA_WIDTH
    o_b = o_z + DELTA_WIDTH
    w_perm = jnp.concatenate(
        [w[:, :o_wi], w[:, o_b:o_b + 2 * DELTA_HEADS],
         jnp.zeros((d, COL_DQKV - (o_wi + 2 * DELTA_HEADS)), w.dtype), w[:, o_wi:o_b]], axis=1).astype(MXU_DTYPE)
    p_p = norm_matmul(x_groups[0], norm_mix[l], w_perm, tm=_token_tile(n_p, 1280), tn=1024, out_dtype=F32,
                      return_normed=False).reshape(b, t, PROJ_PAD)
    p_s = norm_matmul(x_groups[1], norm_mix[l], w_perm, tm=_token_tile(n_s, 1280), tn=1024, out_dtype=F32,
                      return_normed=False).reshape(db, ds, PROJ_PAD)

    pos_p = jnp.arange(t)
    tp = _round_up(t, ATTN_QBLOCK)
    topk_p = min(TOPK_MAX, (t - N_META) // 4)
    qkv, k_p, v_p = rope_cast(p_p, pos_p, tp=tp)
    k_p = k_p.reshape(b, t, ATTN_HEADS, HEAD_DIM)
    v_p = v_p.reshape(b, t, ATTN_HEADS, HEAD_DIM)
    qi, ki_p, wi, g, beta = _small_projection(p_p, pos_p, a_log[l], dt_bias[l])
    attn_p = attn_prompt(qkv, _pad_rows(qi, tp).transpose(0, 2, 1, 3).astype(MXU_DTYPE),
                         _pad_rows(ki_p, tp).astype(MXU_DTYPE), _pad_rows(wi, tp),
                         tq=ATTN_QBLOCK, topk=topk_p)[:, :t]
    conv_prev_p = jnp.zeros((b, CONV_WIDTH - 1, 3 * DELTA_WIDTH), F32)
    qkvz = delta_prep(p_p, conv_prev_p, conv_w[l], tp=_delta_rows(t)[1])
    conv_p = jnp.concatenate([conv_prev_p, p_p[:, -(CONV_WIDTH - 1):, COL_DQKV:COL_Z]], axis=1)[:, -(CONV_WIDTH - 1):]
    gated_p, ssm_p = _delta_group(qkvz, g, beta, jnp.zeros((b, DELTA_HEADS, DELTA_DK, DELTA_DV), F32), delta_norm[l])

    pos_s = past + jnp.arange(ds)
    (q, k_s, v_s, qi, ki_s, wi), (dq, dk, dv, g, beta, z), conv_s = _split_projection(
        p_s, pos_s, state_conv[l], conv_w[l], a_log[l], dt_bias[l])
    topk_s = min(TOPK_MAX, (past + ds) // 4)
    attn_s = _sample_attention(q, k_s, v_s, qi, ki_s, wi, cache_k[l], cache_v[l], cache_kidx[l],
                               page_table, topk_s).astype(MXU_DTYPE)
    tp_s = _delta_rows(ds)[1]
    gated_s, ssm_s = _delta_group(jnp.stack([_pad_rows(a_, tp_s) for a_ in (dq, dk, dv, z)]), g, beta,
                                  state_ssm[l], delta_norm[l])

    def flat(xp_, xs_):
        return jnp.concatenate([xp_.reshape(n_p, -1), xs_.reshape(n_s, -1),
                                jnp.zeros((n_pad - n_p - n_s, xp_.shape[-1]), xp_.dtype)], axis=0)

    wo = w_out[l].astype(MXU_DTYPE)
    x1 = matmul2_residual(flat(attn_p, attn_s), flat(gated_p, gated_s), wo[:ATTN_WIDTH], wo[ATTN_WIDTH:], x,
                          tm=tm, tn=1024 if d % 1024 == 0 else d)

    wq = peer_wq[l].astype(MXU_DTYPE)
    qp, xn = norm_matmul(x1, norm_ffn[l], wq, tm=tm, tn=wq.shape[1], out_dtype=MXU_DTYPE, return_normed=True)
    qp = qp.reshape(n_pad, 2 * PEER_HEADS, PEER_QDIM // 2).transpose(1, 0, 2)
    a, bb, misc = peer_select(qp, peer_subkeys[l].astype(MXU_DTYPE), tn=TOKEN_BLOCK)
    y = peer_dense(xn, a, bb, misc, peer_u[l].astype(MXU_DTYPE), peer_v[l].T.astype(MXU_DTYPE), x1, norm_out,
                   tn=TOKEN_BLOCK, te=PEER_EXPERT_BLOCK)
    caches_p = (k_p, v_p, ki_p, conv_p, ssm_p)
    caches_s = (k_s, v_s, ki_s, conv_s, ssm_s)
    return y, caches_p, caches_s


def kernel(x_prompt, x_sample, cache_k, cache_v, cache_kidx, state_conv, state_ssm, page_table, meta, norm_mix,
           w_in, conv_w, a_log, dt_bias, delta_norm, w_out, norm_ffn, peer_wq, peer_subkeys, peer_u, peer_v,
           norm_final):
    b, s, d = x_prompt.shape
    db, ds, _ = x_sample.shape
    t = s + N_META
    depth = w_in.shape[0]
    assert depth == 1, "the fused PEER + final-norm epilogue assumes a single layer"
    n_p, n_s = b * t, db * ds
    n_pad = _round_up(n_p + n_s, TOKEN_BLOCK)
    xp = jnp.concatenate([jnp.broadcast_to(meta[None].astype(F32), (b, N_META, d)), x_prompt], axis=1)
    x = jnp.concatenate([xp.reshape(n_p, d), x_sample.reshape(n_s, d), jnp.zeros((n_pad - n_p - n_s, d), F32)], axis=0)
    y, cp, cs = _layer(0, x, (xp.reshape(n_p, d), x_sample.reshape(n_s, d)), (b, t, db, ds), cache_k, cache_v, cache_kidx, state_conv, state_ssm, page_table,
                       norm_mix, w_in, conv_w, a_log, dt_bias, delta_norm, w_out, norm_ffn,
                       peer_wq, peer_subkeys, peer_u, peer_v, norm_final)
    y_prompt = y[:n_p].reshape(b, t, d)[:, N_META:]
    y_sample = y[n_p:n_p + n_s].reshape(db, ds, d)
    return (y_prompt, y_sample) + tuple(c[None] for c in cp) + tuple(c[None] for c in cs)
```

```python
import functools
import math

import jax
import jax.numpy as jnp
from jax import lax
from jax.experimental import pallas as pl
from jax.experimental.pallas import tpu as pltpu

F32 = jnp.float32
MXU_DTYPE = jnp.bfloat16

N_META = 16
HEAD_DIM = 128
ATTN_HEADS = 8
ATTN_WIDTH = ATTN_HEADS * HEAD_DIM
IDX_HEADS = 8
IDX_DIM = 64
TOPK_MAX = 256
ROPE_THETA = 500000.0
DELTA_DK = 128
DELTA_DV = 128
DELTA_HEADS = 8
DELTA_WIDTH = DELTA_HEADS * DELTA_DV
CONV_WIDTH = 4
PEER_HEADS = 8
PEER_NKEYS = 128
PEER_QDIM = 256
PEER_TOPK = 16
PAGE_SIZE = 128
EPS = 1e-6
NEG = -1e30

COL_Q = 0
COL_K = COL_Q + ATTN_WIDTH
COL_V = COL_K + ATTN_WIDTH
COL_QI = COL_V + ATTN_WIDTH
COL_SMALL = COL_QI + IDX_HEADS * IDX_DIM
COL_DQKV = 4096
COL_Z = COL_DQKV + 3 * DELTA_WIDTH
COL_END = COL_Z + DELTA_WIDTH
PROJ_PAD = COL_END
assert COL_SMALL + 128 <= COL_DQKV and COL_DQKV % DELTA_WIDTH == 0

LANES = 128
TOKEN_BLOCK = 512
ATTN_QBLOCK = 256
DELTA_CHUNK = 128
DELTA_HEADS_PER_STEP = 4
SCORE_PAGES_PER_STEP = 8
ATTEND_PAGES_PER_STEP = 8
PEER_EXPERT_BLOCK = 1024
VMEM_LIMIT = 56 * 1024 * 1024

_NT = (((1,), (1,)), ((), ()))


def _round_up(x, m):
    return (x + m - 1) // m * m


def _cparams(*sem):
    return pltpu.CompilerParams(dimension_semantics=sem, vmem_limit_bytes=VMEM_LIMIT)


def _norm_matmul_kernel(x_ref, g_ref, w_ref, o_ref, xn_ref):
    @pl.when(pl.program_id(1) == 0)
    def _():
        x = x_ref[...]
        ms = jnp.mean(x * x, axis=-1, keepdims=True)
        xn_ref[...] = (x * lax.rsqrt(ms + EPS) * g_ref[...]).astype(xn_ref.dtype)

    o_ref[...] = jnp.dot(xn_ref[...], w_ref[...], preferred_element_type=F32).astype(o_ref.dtype)


def norm_matmul(x, gain, w, *, tm, tn, out_dtype, return_normed):
    n, d = x.shape
    width = w.shape[1]
    grid = (n // tm, width // tn)
    in_specs = [
        pl.BlockSpec((tm, d), lambda i, j: (i, 0)),
        pl.BlockSpec((1, d), lambda i, j: (0, 0)),
        pl.BlockSpec((d, tn), lambda i, j: (0, j)),
    ]
    o_spec = pl.BlockSpec((tm, tn), lambda i, j: (i, j))
    o_shape = jax.ShapeDtypeStruct((n, width), out_dtype)
    xn_spec = pl.BlockSpec((tm, d), lambda i, j: (i, 0))
    if return_normed:
        return pl.pallas_call(
            _norm_matmul_kernel, grid=grid, in_specs=in_specs,
            out_specs=[o_spec, xn_spec],
            out_shape=[o_shape, jax.ShapeDtypeStruct((n, d), MXU_DTYPE)],
            compiler_params=_cparams("parallel", "arbitrary"), name="norm_matmul_xn",
        )(x, gain.reshape(1, d).astype(F32), w)
    return pl.pallas_call(
        _norm_matmul_kernel, grid=grid, in_specs=in_specs, out_specs=o_spec, out_shape=o_shape,
        scratch_shapes=[pltpu.VMEM((tm, d), MXU_DTYPE)],
        compiler_params=_cparams("parallel", "arbitrary"), name="norm_matmul",
    )(x, gain.reshape(1, d).astype(F32), w)


def _matmul2_kernel(a1_ref, a2_ref, w1_ref, w2_ref, r_ref, o_ref):
    acc = jnp.dot(a1_ref[...], w1_ref[...], preferred_element_type=F32)
    acc = acc + jnp.dot(a2_ref[...], w2_ref[...], preferred_element_type=F32)
    o_ref[...] = acc + r_ref[...]


def matmul2_residual(a1, a2, w1, w2, r, *, tm, tn):
    n, k1 = a1.shape
    k2 = a2.shape[1]
    d = w1.shape[1]
    return pl.pallas_call(
        _matmul2_kernel, grid=(n // tm, d // tn),
        in_specs=[
            pl.BlockSpec((tm, k1), lambda i, j: (i, 0)),
            pl.BlockSpec((tm, k2), lambda i, j: (i, 0)),
            pl.BlockSpec((k1, tn), lambda i, j: (0, j)),
            pl.BlockSpec((k2, tn), lambda i, j: (0, j)),
            pl.BlockSpec((tm, tn), lambda i, j: (i, j)),
        ],
        out_specs=pl.BlockSpec((tm, tn), lambda i, j: (i, j)),
        out_shape=jax.ShapeDtypeStruct((n, d), F32),
        compiler_params=_cparams("parallel", "arbitrary"), name="out_proj",
    )(a1, a2, w1, w2, r)


def _order_key(s):
    s = jnp.where(s == 0.0, 0.0, s)
    key = lax.bitcast_convert_type(s, jnp.int32)
    return jnp.where(key < 0, key ^ jnp.int32(0x7FFFFFFF), key)


def _count_lanes(mask):
    return jnp.sum(jnp.where(mask, 1.0, 0.0), axis=-1, keepdims=True)


def _count_pages_lanes(mask):
    per_query = jnp.sum(jnp.where(mask, 1.0, 0.0), axis=0, keepdims=True)
    return jnp.sum(per_query, axis=2, keepdims=True)


def _topk_threshold(key, col, topk, idx_bits, count):
    kf = float(topk)
    nonneg = count(key >= 0) >= kf
    base = jnp.where(nonneg, jnp.int32(0), jnp.int32(-2 ** 31))

    def value_bit(it, base):
        cand = base | jnp.left_shift(jnp.int32(1), 30 - it)
        return jnp.where(count(key >= cand) >= kf, cand, base)

    thr = lax.fori_loop(0, 31, value_bit, base)
    tied = key == thr
    need = kf - count(key > thr)

    def index_bit(it, j):
        cand = j | jnp.left_shift(jnp.int32(1), idx_bits - 1 - it)
        return jnp.where(count(tied & (col < cand)) < need, cand, j)

    j = lax.fori_loop(0, idx_bits, index_bit, jnp.zeros_like(thr))
    return thr, j


def _attn_prompt_kernel(q_ref, k_ref, v_ref, qi_ref, ki_ref, wi_ref, o_ref, *, q_start, topk, idx_bits):
    tq = q_ref.shape[1]
    kl = k_ref.shape[1]
    ki = ki_ref[0]
    wi = wi_ref[0]
    s = jnp.zeros((tq, kl), F32)
    for h in range(IDX_HEADS):
        d = lax.dot_general(qi_ref[0, h], ki, _NT, preferred_element_type=F32)
        s = s + jnp.maximum(d, 0.0) * wi[:, h:h + 1]
    row = q_start + lax.broadcasted_iota(jnp.int32, (tq, kl), 0)
    col = lax.broadcasted_iota(jnp.int32, (tq, kl), 1)
    causal = col <= row
    key = _order_key(jnp.where(causal, s, NEG))
    thr, j = _topk_threshold(key, col, topk, idx_bits, _count_lanes)
    sel = (key > thr) | ((key == thr) & (col <= j))
    bias = jnp.where(sel & causal, 0.0, NEG)
    scale = HEAD_DIM ** -0.5
    for h in range(ATTN_HEADS):
        hs = slice(h * HEAD_DIM, (h + 1) * HEAD_DIM)
        logits = lax.dot_general(q_ref[0, :, hs], k_ref[0, :, hs], _NT, preferred_element_type=F32) * scale + bias
        m = jnp.max(logits, axis=-1, keepdims=True)
        p = jnp.exp(logits - m)
        l = jnp.sum(p, axis=-1, keepdims=True)
        o = jnp.dot(p.astype(v_ref.dtype), v_ref[0, :, hs], preferred_element_type=F32)
        o_ref[0, :, hs] = (o / l).astype(o_ref.dtype)


def attn_prompt(qkv, qi, ki, wi, *, tq, topk):
    _, b, tp, w = qkv.shape
    outs = []
    for i in range(tp // tq):
        kl = (i + 1) * tq
        kern = functools.partial(_attn_prompt_kernel, q_start=i * tq, topk=topk,
                                 idx_bits=max(1, (kl - 1).bit_length()))
        outs.append(pl.pallas_call(
            kern, grid=(b,),
            in_specs=[
                pl.BlockSpec((None, 1, tq, w), lambda bb, i=i: (0, bb, i, 0)),
                pl.BlockSpec((None, 1, kl, w), lambda bb: (1, bb, 0, 0)),
                pl.BlockSpec((None, 1, kl, w), lambda bb: (2, bb, 0, 0)),
                pl.BlockSpec((1, IDX_HEADS, tq, IDX_DIM), lambda bb, i=i: (bb, 0, i, 0)),
                pl.BlockSpec((1, kl, IDX_DIM), lambda bb: (bb, 0, 0)),
                pl.BlockSpec((1, tq, IDX_HEADS), lambda bb, i=i: (bb, i, 0)),
            ],
            out_specs=pl.BlockSpec((1, tq, w), lambda bb: (bb, 0, 0)),
            out_shape=jax.ShapeDtypeStruct((b, tq, w), MXU_DTYPE),
            compiler_params=_cparams("parallel"), name=f"attn_prompt_q{i}",
        )(qkv, qkv, qkv, qi, ki, wi))
    return jnp.concatenate(outs, axis=1)


def _page_scores(qi, wcol, kpage, ds):
    d = lax.dot_general(qi, kpage.astype(qi.dtype), _NT, preferred_element_type=F32)
    r = jnp.maximum(d, 0.0) * wcol
    return jnp.sum(r.reshape(ds, IDX_HEADS, r.shape[-1]), axis=1)


def _sample_scores_kernel(pt_ref, qi_ref, wi_ref, knew_ref, *rest, n_pages_step, ds):
    page_refs = rest[:n_pages_step]
    past_ref, new_ref = rest[n_pages_step:]
    qi = qi_ref[0]
    wcol = wi_ref[0]
    for g in range(n_pages_step):
        past_ref[0, g] = _page_scores(qi, wcol, page_refs[g][0], ds)
    s_new = _page_scores(qi, wcol, knew_ref[0], ds)
    qrow = lax.broadcasted_iota(jnp.int32, s_new.shape, 0)
    kcol = lax.broadcasted_iota(jnp.int32, s_new.shape, 1)
    new_ref[0] = jnp.where(kcol <= qrow, s_new, NEG)


def sample_scores(qi, wi, ki_new_page, cache_kidx, page_table, *, ds):
    db, n_pages = page_table.shape
    g = math.gcd(SCORE_PAGES_PER_STEP, n_pages)
    rows = ds * IDX_HEADS
    page_specs = [pl.BlockSpec((1, PAGE_SIZE, IDX_DIM), lambda b, p, pt, gg=gg: (pt[b, p * g + gg], 0, 0))
                  for gg in range(g)]
    grid_spec = pltpu.PrefetchScalarGridSpec(
        num_scalar_prefetch=1, grid=(db, n_pages // g),
        in_specs=[pl.BlockSpec((1, rows, IDX_DIM), lambda b, p, pt: (b, 0, 0)),
                  pl.BlockSpec((1, rows, 1), lambda b, p, pt: (b, 0, 0)),
                  pl.BlockSpec((1, PAGE_SIZE, IDX_DIM), lambda b, p, pt: (b, 0, 0))] + page_specs,
        out_specs=[pl.BlockSpec((1, g, ds, PAGE_SIZE), lambda b, p, pt: (b, p, 0, 0)),
                   pl.BlockSpec((1, ds, PAGE_SIZE), lambda b, p, pt: (b, 0, 0))])
    return pl.pallas_call(
        functools.partial(_sample_scores_kernel, n_pages_step=g, ds=ds), grid_spec=grid_spec,
        out_shape=[jax.ShapeDtypeStruct((db, n_pages, ds, PAGE_SIZE), F32),
                   jax.ShapeDtypeStruct((db, ds, PAGE_SIZE), F32)],
        compiler_params=_cparams("parallel", "arbitrary"), name="sample_scores",
    )(page_table, qi, wi, ki_new_page, *([cache_kidx] * g))


def _sample_attend_kernel(pt_ref, sc_ref, scn_ref, q_ref, kn_ref, vn_ref, *rest, n_pages_step, ds, topk, idx_bits):
    k_refs = rest[:n_pages_step]
    v_refs = rest[n_pages_step:2 * n_pages_step]
    o_ref, thr_ref, j_ref, m_ref, l_ref, acc_ref = rest[2 * n_pages_step:]
    p = pl.program_id(1)
    n_pages = sc_ref.shape[1]
    nh = ATTN_HEADS
    scale = HEAD_DIM ** -0.5
    group_w = ds * nh
    pack = LANES // group_w
    page_rows = PAGE_SIZE * nh
    sub = lax.broadcasted_iota(jnp.int32, (nh, LANES), 0)
    lane = lax.broadcasted_iota(jnp.int32, (nh, LANES), 1)
    head_bias = jnp.where(lane % nh == sub, 0.0, NEG)
    qrow = lax.broadcasted_iota(jnp.int32, (LANES, LANES), 0)
    qlane = lax.broadcasted_iota(jnp.int32, (LANES, LANES), 1)
    expand = jnp.where(qlane // nh == qrow, 1.0, 0.0).astype(q_ref.dtype)
    slot_of_lane = lax.broadcasted_iota(jnp.int32, (1, LANES), 1) // group_w

    @pl.when(p == 0)
    def _():
        key = _order_key(jnp.concatenate([sc_ref[0], scn_ref[0][None]], axis=0))
        col = (lax.broadcasted_iota(jnp.int32, key.shape, 0) * PAGE_SIZE
               + lax.broadcasted_iota(jnp.int32, key.shape, 2))
        thr, j = _topk_threshold(key, col, topk, idx_bits, _count_pages_lanes)
        thr_ref[...] = thr[0]
        j_ref[...] = j[0]
        m_ref[...] = jnp.full_like(m_ref, NEG)
        l_ref[...] = jnp.zeros_like(l_ref)
        acc_ref[...] = jnp.zeros_like(acc_ref)

    def selected(scores, page_idx):
        key = _order_key(scores)
        col = page_idx * PAGE_SIZE + lax.broadcasted_iota(jnp.int32, key.shape, 1)
        sel = ((key > thr_ref[...]) | ((key == thr_ref[...]) & (col <= j_ref[...]))) & (scores > 0.5 * NEG)
        return jnp.where(sel, 1.0, 0.0)

    def attend(sels, kpages, vpages):
        empty = jnp.zeros((ds, PAGE_SIZE), F32)
        sel_rows = jnp.concatenate([empty if s is None else s for s in sels]
                                   + [jnp.zeros((LANES - pack * ds, PAGE_SIZE), F32)], axis=0)
        spread = jnp.dot(sel_rows.T.astype(q_ref.dtype), expand, preferred_element_type=F32)
        key_bias = jnp.where(spread > 0.5, 0.0, NEG)
        bias = (key_bias[:, None, :] + head_bias[None, :, :]).reshape(page_rows, LANES)
        kcat = jnp.concatenate([kp.astype(q_ref.dtype) for kp in kpages], axis=1)
        logits = jnp.dot(kcat, q_ref[0], preferred_element_type=F32) * scale + bias
        m_new = jnp.maximum(m_ref[...], jnp.max(logits, axis=0, keepdims=True))
        alpha = jnp.exp(m_ref[...] - m_new)
        pr = jnp.exp(logits - m_new)
        l_ref[...] = alpha * l_ref[...] + jnp.sum(pr, axis=0, keepdims=True)
        vt = jnp.concatenate([vp.T.astype(q_ref.dtype) for vp in vpages], axis=0)
        pv = jnp.dot(vt, pr.astype(q_ref.dtype), preferred_element_type=F32)
        new = jnp.zeros((HEAD_DIM, LANES), F32)
        for s in range(pack):
            new = new + jnp.where(slot_of_lane == s, pv[s * HEAD_DIM:(s + 1) * HEAD_DIM], 0.0)
        acc_ref[...] = alpha * acc_ref[...] + new
        m_ref[...] = m_new

    for tile in range(n_pages_step // pack):
        idx = [p * n_pages_step + tile * pack + s for s in range(pack)]
        attend([selected(sc_ref[0, i], i) for i in idx],
               [k_refs[tile * pack + s][0] for s in range(pack)], [v_refs[tile * pack + s][0] for s in range(pack)])

    @pl.when(p == pl.num_programs(1) - 1)
    def _():
        nothing = jnp.zeros((page_rows, HEAD_DIM), F32)
        attend([selected(scn_ref[0], n_pages)] + [None] * (pack - 1),
               [kn_ref[0]] + [nothing] * (pack - 1), [vn_ref[0]] + [nothing] * (pack - 1))
        m = m_ref[...]
        m_all = m
        for s in range(1, pack):
            m_all = jnp.maximum(m_all, pltpu.roll(m, s * group_w, 1))
        w = jnp.exp(m - m_all)
        l_w = l_ref[...] * w
        a_w = acc_ref[...] * w
        l_all, a_all = l_w, a_w
        for s in range(1, pack):
            l_all = l_all + pltpu.roll(l_w, s * group_w, 1)
            a_all = a_all + pltpu.roll(a_w, s * group_w, 1)
        o_ref[0] = a_all / l_all


def sample_attend(scores, scores_new, q_cols, k_new_page, v_new_page, cache_k, cache_v, page_table, *, ds, topk):
    db, n_pages = page_table.shape
    g = math.gcd(ATTEND_PAGES_PER_STEP, n_pages)
    pack = q_cols.shape[1] // HEAD_DIM
    assert g % pack == 0 and pack * ds * ATTN_HEADS == LANES
    page_rows = PAGE_SIZE * ATTN_HEADS
    idx_bits = max(1, ((n_pages + 1) * PAGE_SIZE - 1).bit_length())

    def page_spec(gg):
        return pl.BlockSpec((1, page_rows, HEAD_DIM), lambda b, p, pt, gg=gg: (pt[b, p * g + gg], 0, 0))

    grid_spec = pltpu.PrefetchScalarGridSpec(
        num_scalar_prefetch=1, grid=(db, n_pages // g),
        in_specs=[pl.BlockSpec((1, n_pages, ds, PAGE_SIZE), lambda b, p, pt: (b, 0, 0, 0)),
                  pl.BlockSpec((1, ds, PAGE_SIZE), lambda b, p, pt: (b, 0, 0)),
                  pl.BlockSpec((1, pack * HEAD_DIM, LANES), lambda b, p, pt: (b, 0, 0)),
                  pl.BlockSpec((1, page_rows, HEAD_DIM), lambda b, p, pt: (b, 0, 0)),
                  pl.BlockSpec((1, page_rows, HEAD_DIM), lambda b, p, pt: (b, 0, 0))]
        + [page_spec(gg) for gg in range(g)] * 2,
        out_specs=pl.BlockSpec((1, HEAD_DIM, LANES), lambda b, p, pt: (b, 0, 0)),
        scratch_shapes=[pltpu.VMEM((ds, 1), jnp.int32), pltpu.VMEM((ds, 1), jnp.int32),
                        pltpu.VMEM((1, LANES), F32), pltpu.VMEM((1, LANES), F32), pltpu.VMEM((HEAD_DIM, LANES), F32)])
    kern = functools.partial(_sample_attend_kernel, n_pages_step=g, ds=ds, topk=topk, idx_bits=idx_bits)
    return pl.pallas_call(
        kern, grid_spec=grid_spec, out_shape=jax.ShapeDtypeStruct((db, HEAD_DIM, LANES), F32),
        compiler_params=_cparams("parallel", "arbitrary"), name="sample_attend",
    )(page_table, scores, scores_new, q_cols, k_new_page, v_new_page, *([cache_k] * g), *([cache_v] * g))


def _split_hi_lo(x):
    hi = x.astype(MXU_DTYPE)
    return hi, (x - hi.astype(F32)).astype(MXU_DTYPE)


def _dot1(a, b, dims=None):
    if dims is None:
        dims = (((1,), (0,)), ((), ()))
    return lax.dot_general(a.astype(MXU_DTYPE), b.astype(MXU_DTYPE), dims, preferred_element_type=F32)


def _dot3(a, b, dims=None):
    ah, al = _split_hi_lo(a)
    bh, bl = _split_hi_lo(b)
    if dims is None:
        dims = (((1,), (0,)), ((), ()))
    d = functools.partial(lax.dot_general, dimension_numbers=dims, preferred_element_type=F32)
    return d(ah, bh) + (d(ah, bl) + d(al, bh))


def _delta_kernel(q_ref, k_ref, v_ref, z_ref, gcum_ref, beta_ref, s0_ref, dn_ref, o_ref, s_ref, t_scr, qk_scr,
                  *, n_chunks):
    c_len = DELTA_CHUNK
    n_h = s0_ref.shape[1]
    ri = lax.broadcasted_iota(jnp.int32, (c_len, c_len), 0)
    ci = lax.broadcasted_iota(jnp.int32, (c_len, c_len), 1)
    incl = ri >= ci
    strict = ri > ci
    eye = jnp.where(ri == ci, 1.0, 0.0)
    dn = dn_ref[...]

    def chunk_rows(c):
        return pl.ds(pl.multiple_of(c * c_len, c_len), c_len)

    def decay_terms(c, hh):
        g_row = jnp.broadcast_to(gcum_ref[0, hh, pl.ds(c, 1), :], (c_len, c_len))
        b_col = jnp.broadcast_to(beta_ref[0, hh, pl.ds(c, 1), :], (c_len, c_len)).T
        return g_row, g_row.T, b_col

    heads = range(n_h)
    head_cols = [slice(hh * DELTA_DK, (hh + 1) * DELTA_DK) for hh in heads]

    def build(c, carry):
        rows = chunk_rows(c)
        qc = [q_ref[0, rows, cs] for cs in head_cols]
        kc = [k_ref[0, rows, cs] for cs in head_cols]
        terms = [decay_terms(c, hh) for hh in heads]
        dec = [jnp.where(incl, jnp.exp(jnp.where(incl, g_col - g_row, 0.0)), 0.0) for g_row, g_col, _ in terms]
        kk = [_dot3(kc[hh], kc[hh], _NT) for hh in heads]
        a = [jnp.where(strict, kk[hh] * dec[hh], 0.0) * terms[hh][2] for hh in heads]
        t = [eye - a[hh] for hh in heads]
        pw = [_dot1(a[hh], a[hh]) for hh in heads]
        span = 2
        while span < c_len:
            t = [t[hh] + _dot1(t[hh], pw[hh]) for hh in heads]
            span *= 2
            if span < c_len:
                pw = [_dot1(pw[hh], pw[hh]) for hh in heads]
        at = [_dot3(a[hh], t[hh]) for hh in heads]
        corr = [_dot3(t[hh], eye - t[hh] - at[hh]) for hh in heads]
        qk = [_dot1(qc[hh], kc[hh], _NT) for hh in heads]
        for hh in heads:
            t_scr[hh, c] = t[hh] + corr[hh]
            qk_scr[hh, c] = qk[hh] * dec[hh]
        return carry

    lax.fori_loop(0, n_chunks, build, 0)

    def step(c, states):
        rows = chunk_rows(c)
        qc = [q_ref[0, rows, cs] for cs in head_cols]
        kc = [k_ref[0, rows, cs] for cs in head_cols]
        terms = [decay_terms(c, hh) for hh in heads]
        e_g = [jnp.exp(g_col) for _, g_col, _ in terms]
        ks = [_dot1(kc[hh], states[hh]) for hh in heads]
        qs = [_dot1(qc[hh], states[hh]) for hh in heads]
        u = [_dot3(t_scr[hh, c], terms[hh][2] * (v_ref[0, rows, head_cols[hh]] - e_g[hh] * ks[hh])) for hh in heads]
        qku = [_dot1(qk_scr[hh, c], u[hh]) for hh in heads]
        g_last = [g_row[:, c_len - 1:c_len] for g_row, _, _ in terms]
        kw = [kc[hh] * jnp.exp(g_last[hh] - terms[hh][1]) for hh in heads]
        upd = [_dot1(kw[hh].T, u[hh]) for hh in heads]
        for hh in heads:
            o = e_g[hh] * qs[hh] + qku[hh]
            z = z_ref[0, rows, head_cols[hh]]
            on = o * lax.rsqrt(jnp.mean(o * o, axis=-1, keepdims=True) + EPS) * dn
            o_ref[0, rows, head_cols[hh]] = (on * (z / (1.0 + jnp.exp(-z)))).astype(o_ref.dtype)
        return tuple(jnp.exp(g_last[hh]) * states[hh] + upd[hh] for hh in heads)

    final = lax.fori_loop(0, n_chunks, step, tuple(s0_ref[0, hh] for hh in range(n_h)))
    for hh in range(n_h):
        s_ref[0, hh] = final[hh]


def gated_delta(qkvz, gcum, beta, s0, delta_norm, *, n_chunks):
    _, b, tp, _ = qkvz.shape
    h = DELTA_HEADS
    n_h = DELTA_HEADS_PER_STEP
    tok = pl.BlockSpec((1, tp, n_h * DELTA_DV), lambda bb, hh: (bb, 0, hh))
    ops = [pl.BlockSpec((None, 1, tp, n_h * DELTA_DV), lambda bb, hh, g=g: (g, bb, 0, hh)) for g in range(4)]
    row = pl.BlockSpec((1, n_h, n_chunks, DELTA_CHUNK), lambda bb, hh: (bb, hh, 0, 0))
    st = pl.BlockSpec((1, n_h, DELTA_DK, DELTA_DV), lambda bb, hh: (bb, hh, 0, 0))
    mat = pltpu.VMEM((n_h, n_chunks, DELTA_CHUNK, DELTA_CHUNK), F32)
    return pl.pallas_call(
        functools.partial(_delta_kernel, n_chunks=n_chunks), grid=(b, h // n_h),
        in_specs=ops + [row, row, st, pl.BlockSpec((1, DELTA_DV), lambda bb, hh: (0, 0))],
        out_specs=[tok, st],
        out_shape=[jax.ShapeDtypeStruct((b, tp, h * DELTA_DV), MXU_DTYPE),
                   jax.ShapeDtypeStruct((b, h, DELTA_DK, DELTA_DV), F32)],
        scratch_shapes=[mat, mat],
        compiler_params=_cparams("parallel", "parallel"), name="gated_delta",
    )(qkvz, qkvz, qkvz, qkvz, gcum, beta, s0, delta_norm.reshape(1, DELTA_DV).astype(F32))


def _top_values(cur, n):
    vals = []
    for r in range(n):
        m = jnp.max(cur, axis=0, keepdims=True)
        vals.append(m)
        if r + 1 < n:
            cur = jnp.where(cur == m, -jnp.inf, cur)
    return vals


_PEER_RANKS = PEER_TOPK + 1
_PEER_PAIRS = [(r, c) for r in range(_PEER_RANKS) for c in range(_PEER_RANKS) if (r + 1) * (c + 1) <= _PEER_RANKS]


def _peer_select_kernel(q_ref, sk_ref, a_ref, b_ref, misc_ref):
    tn = q_ref.shape[1]

    def head(h, carry):
        shifted = []
        tops = []
        for c in range(2):
            st = lax.dot_general(sk_ref[c], q_ref[h * 2 + c], _NT, preferred_element_type=F32)
            vals = _top_values(st, _PEER_RANKS)
            shifted.append(st - vals[0])
            tops.append([v - vals[0] for v in vals])
        a_ref[h] = shifted[0]
        b_ref[h] = shifted[1]
        cand = [tops[0][r] + tops[1][c] for r, c in _PEER_PAIRS]
        pad = _round_up(len(cand), 8) - len(cand)
        cand = jnp.concatenate(cand + [jnp.full((pad, tn), -jnp.inf, F32)], axis=0)
        best = _top_values(cand, _PEER_RANKS)
        zsum = jnp.zeros_like(best[0])
        for v in best[:PEER_TOPK]:
            zsum = zsum + jnp.exp(v)
        thr = 0.5 * (best[PEER_TOPK - 1] + best[PEER_TOPK])
        misc_ref[h] = jnp.concatenate([thr, 1.0 / zsum, jnp.zeros((6, tn), F32)], axis=0)
        return carry

    lax.fori_loop(0, PEER_HEADS, head, 0)


def peer_select(q, subkeys, *, tn):
    n = q.shape[1]
    blk = pl.BlockSpec((PEER_HEADS, PEER_NKEYS, tn), lambda i: (0, 0, i))
    return pl.pallas_call(
        _peer_select_kernel, grid=(n // tn,),
        in_specs=[pl.BlockSpec((q.shape[0], tn, q.shape[2]), lambda i: (0, i, 0)),
                  pl.BlockSpec(subkeys.shape, lambda i: (0, 0, 0))],
        out_specs=[blk, blk, pl.BlockSpec((PEER_HEADS, 8, tn), lambda i: (0, 0, i))],
        out_shape=[jax.ShapeDtypeStruct((PEER_HEADS, PEER_NKEYS, n), F32),
                   jax.ShapeDtypeStruct((PEER_HEADS, PEER_NKEYS, n), F32),
                   jax.ShapeDtypeStruct((PEER_HEADS, 8, n), F32)],
        compiler_params=_cparams("parallel"), name="peer_select",
    )(q, subkeys)


def _gelu_tanh(x):
    return 0.5 * x * (1.0 + jnp.tanh(math.sqrt(2.0 / math.pi) * (x + 0.044715 * (x * x * x))))


def _peer_dense_kernel(xn_ref, a_ref, b_ref, misc_ref, u_ref, vt_ref, x_ref, gf_ref, o_ref,
                       acc_ref, bexp_ref, g_ref, row_ref, w_ref):
    e = pl.program_id(1)
    te = u_ref.shape[0]
    tn = xn_ref.shape[0]
    nk = PEER_NKEYS

    @pl.when(e == 0)
    def _():
        acc_ref[...] = jnp.zeros_like(acc_ref)
        for h in range(PEER_HEADS):
            bexp_ref[h] = jnp.exp(b_ref[h])

    n_i = te // nk

    thr_all = jnp.concatenate([misc_ref[h, 0:1, :] for h in range(PEER_HEADS)], axis=0)
    invz_all = jnp.concatenate([misc_ref[h, 1:2, :] for h in range(PEER_HEADS)], axis=0)
    for ii in range(n_i):
        a_rows = jnp.concatenate([a_ref[h, pl.ds(e * n_i + ii, 1), :] for h in range(PEER_HEADS)], axis=0)
        row_ref[0, ii] = thr_all - a_rows
        row_ref[1, ii] = jnp.exp(a_rows) * invz_all

    key_tile = 32
    for tb in range(tn // LANES):
        lanes = slice(tb * LANES, (tb + 1) * LANES)
        for kt in range(nk // key_tile):
            keys = slice(kt * key_tile, (kt + 1) * key_tile)
            w = [None] * n_i
            for h in range(PEER_HEADS):
                b_tile = b_ref[h, keys, lanes]
                bexp_tile = bexp_ref[h, keys, lanes]
                for ii in range(n_i):
                    wh = (jnp.where(b_tile >= row_ref[0, ii, h:h + 1, lanes], bexp_tile, 0.0)
                          * row_ref[1, ii, h:h + 1, lanes])
                    w[ii] = wh if w[ii] is None else w[ii] + wh
            for ii in range(n_i):
                w_ref[ii * nk + kt * key_tile:ii * nk + (kt + 1) * key_tile, lanes] = w[ii]

    up_rows = min(te, 256)
    for r in range(te // up_rows):
        rows = slice(r * up_rows, (r + 1) * up_rows)
        ht = lax.dot_general(u_ref[rows, :], xn_ref[...], _NT, preferred_element_type=F32)
        g_ref[rows, :] = (w_ref[rows, :] * _gelu_tanh(ht)).astype(g_ref.dtype)
    acc_ref[...] += jnp.dot(vt_ref[...], g_ref[...], preferred_element_type=F32)

    @pl.when(e == pl.num_programs(1) - 1)
    def _():
        y = acc_ref[...].T + x_ref[...]
        ms = jnp.mean(y * y, axis=-1, keepdims=True)
        o_ref[...] = y * lax.rsqrt(ms + EPS) * gf_ref[...]


def peer_dense(xn, a, b, misc, u, vt, x, gain_final, *, tn, te):
    n, d = x.shape
    n_blocks = u.shape[0] // te
    once = pl.Buffered(1)
    sel = pl.BlockSpec((PEER_HEADS, PEER_NKEYS, tn), lambda i, e: (0, 0, i), pipeline_mode=once)
    return pl.pallas_call(
        _peer_dense_kernel, grid=(n // tn, n_blocks),
        in_specs=[
            pl.BlockSpec((tn, d), lambda i, e: (i, 0), pipeline_mode=once),
            sel, sel,
            pl.BlockSpec((PEER_HEADS, 8, tn), lambda i, e: (0, 0, i)),
            pl.BlockSpec((te, d), lambda i, e: (e, 0)),
            pl.BlockSpec((d, te), lambda i, e: (0, e)),
            pl.BlockSpec((tn, d), lambda i, e: (i, 0), pipeline_mode=once),
            pl.BlockSpec((1, d), lambda i, e: (0, 0)),
        ],
        out_specs=pl.BlockSpec((tn, d), lambda i, e: (i, 0)),
        out_shape=jax.ShapeDtypeStruct((n, d), F32),
        scratch_shapes=[pltpu.VMEM((d, tn), F32),
                        pltpu.VMEM((PEER_HEADS, PEER_NKEYS, tn), F32),
                        pltpu.VMEM((te, tn), MXU_DTYPE),
                        pltpu.VMEM((2, te // PEER_NKEYS, PEER_HEADS, tn), F32),
                        pltpu.VMEM((te, tn), F32)],
        compiler_params=_cparams("parallel", "arbitrary"), name="peer_dense",
    )(xn, a, b, misc, u, vt, x, gain_final.reshape(1, d).astype(F32))


PREP_LANES = 256


def _rope_cast_kernel(x_ref, c_ref, s1_ref, s2_ref, obf_ref, kf_ref, vf_ref):
    g = pl.program_id(2)
    t = x_ref.shape[1]
    tp = obf_ref.shape[2]
    x = x_ref[0]

    def emit(y):
        obf_ref[0, 0, :t, :] = y.astype(obf_ref.dtype)
        if tp > t:
            obf_ref[0, 0, t:, :] = jnp.zeros((tp - t, y.shape[1]), obf_ref.dtype)

    def rotary():
        parts = []
        for h in range(x.shape[1] // HEAD_DIM):
            xs = x[:, h * HEAD_DIM:(h + 1) * HEAD_DIM]
            half = HEAD_DIM // 8
            parts.append(xs * c_ref[...] + pltpu.roll(xs, half, 1) * s1_ref[...]
                         + pltpu.roll(xs, HEAD_DIM - half, 1) * s2_ref[...])
        return jnp.concatenate(parts, axis=1)

    @pl.when(g == 0)
    def _():
        emit(rotary())

    @pl.when(g == 1)
    def _():
        y = rotary()
        kf_ref[0] = y
        emit(y)

    @pl.when(g == 2)
    def _():
        vf_ref[0] = x
        emit(x)


def rope_cast(p, pos, *, tp):
    b, t, _ = p.shape
    rot = HEAD_DIM // 4
    half = rot // 2
    inv = ROPE_THETA ** (-jnp.arange(half, dtype=F32) * 2.0 / rot)
    ang = pos.astype(F32)[:, None] * inv[None, :]
    cos, sin, zero = jnp.cos(ang), jnp.sin(ang), jnp.zeros((t, half), F32)
    rest = HEAD_DIM - rot
    c = jnp.concatenate([cos, cos, jnp.ones((t, rest), F32)], axis=1)
    s1 = jnp.concatenate([zero, sin, jnp.zeros((t, rest), F32)], axis=1)
    s2 = jnp.concatenate([-sin, zero, jnp.zeros((t, rest), F32)], axis=1)
    per_group = ATTN_WIDTH // PREP_LANES
    tab = pl.BlockSpec((t, HEAD_DIM), lambda bb, j, g: (0, 0))
    f32_out = pl.BlockSpec((1, t, PREP_LANES), lambda bb, j, g: (bb, 0, j))
    return pl.pallas_call(
        _rope_cast_kernel, grid=(b, per_group, 3),
        in_specs=[pl.BlockSpec((1, t, PREP_LANES), lambda bb, j, g: (bb, 0, g * per_group + j)), tab, tab, tab],
        out_specs=[pl.BlockSpec((1, 1, tp, PREP_LANES), lambda bb, j, g: (g, bb, 0, j)), f32_out, f32_out],
        out_shape=[jax.ShapeDtypeStruct((3, b, tp, ATTN_WIDTH), MXU_DTYPE),
                   jax.ShapeDtypeStruct((b, t, ATTN_WIDTH), F32), jax.ShapeDtypeStruct((b, t, ATTN_WIDTH), F32)],
        compiler_params=_cparams("parallel", "parallel", "arbitrary"), name="rope_cast",
    )(p, c, s1, s2)


def _delta_prep_kernel(x_ref, prev_ref, w_ref, o_ref):
    g = pl.program_id(1)
    t = x_ref.shape[1]
    tp = o_ref.shape[2]
    x = x_ref[0]

    def emit(y):
        o_ref[0, 0, :t, :] = y
        if tp > t:
            o_ref[0, 0, t:, :] = jnp.zeros((tp - t, y.shape[1]), F32)

    def conv_silu():
        w = w_ref[...]
        head = jnp.concatenate([prev_ref[0], x[:8]], axis=0)
        acc = x * w[CONV_WIDTH - 1:CONV_WIDTH]
        for s in range(1, CONV_WIDTH):
            shifted = jnp.concatenate([head[8 - s:16 - s], pltpu.roll(x, s, 0)[8:]], axis=0)
            acc = acc + shifted * w[CONV_WIDTH - 1 - s:CONV_WIDTH - s]
        return acc * (1.0 / (1.0 + jnp.exp(-acc)))

    @pl.when(g < 2)
    def _():
        y = conv_silu()
        scale = jnp.where(g == 0, DELTA_DK ** -0.5, 1.0)
        parts = []
        for h in range(y.shape[1] // DELTA_DK):
            ys = y[:, h * DELTA_DK:(h + 1) * DELTA_DK]
            parts.append(ys * lax.rsqrt(jnp.sum(ys * ys, axis=-1, keepdims=True) + EPS) * scale)
        emit(jnp.concatenate(parts, axis=1))

    @pl.when(g == 2)
    def _():
        emit(conv_silu())

    @pl.when(g == 3)
    def _():
        emit(x)


def delta_prep(p, conv_prev, conv_w, *, tp):
    b, t, _ = p.shape
    assert t >= 8
    per_group = DELTA_WIDTH // PREP_LANES
    first = COL_DQKV // PREP_LANES
    prev8 = jnp.pad(conv_prev.astype(F32), ((0, 0), (8 - (CONV_WIDTH - 1), 0), (0, 0)))

    def conv_col(g, j):
        return jnp.minimum(g, 2) * per_group + j

    return pl.pallas_call(
        _delta_prep_kernel, grid=(b, 4, per_group),
        in_specs=[pl.BlockSpec((1, t, PREP_LANES), lambda bb, g, j: (bb, 0, first + g * per_group + j)),
                  pl.BlockSpec((1, 8, PREP_LANES), lambda bb, g, j: (bb, 0, conv_col(g, j))),
                  pl.BlockSpec((CONV_WIDTH, PREP_LANES), lambda bb, g, j: (0, conv_col(g, j)))],
        out_specs=pl.BlockSpec((1, 1, tp, PREP_LANES), lambda bb, g, j: (g, bb, 0, j)),
        out_shape=jax.ShapeDtypeStruct((4, b, tp, DELTA_WIDTH), F32),
        compiler_params=_cparams("parallel", "parallel", "parallel"), name="delta_prep",
    )(p, prev8, conv_w.astype(F32))


def _rope(x, pos):
    rot = x.shape[-1] // 4
    half = rot // 2
    inv = ROPE_THETA ** (-jnp.arange(half, dtype=F32) * 2.0 / rot)
    ang = pos.astype(F32)[:, None] * inv[None, :]
    cos = jnp.cos(ang)[None, :, None, :]
    sin = jnp.sin(ang)[None, :, None, :]
    x1 = x[..., :half]
    x2 = x[..., half:rot]
    return jnp.concatenate([x1 * cos - x2 * sin, x2 * cos + x1 * sin, x[..., rot:]], axis=-1)


def _l2norm(x):
    return x * lax.rsqrt(jnp.sum(x * x, axis=-1, keepdims=True) + EPS)


def _small_projection(p, pos, a_log, dt_bias):
    b, t, _ = p.shape
    qi = _rope(p[..., COL_QI:COL_SMALL].reshape(b, t, IDX_HEADS, IDX_DIM), pos)
    ki = _rope(p[..., COL_SMALL:COL_SMALL + IDX_DIM][:, :, None, :], pos)[:, :, 0, :]
    o = COL_SMALL + IDX_DIM
    wi = p[..., o:o + IDX_HEADS] * (IDX_HEADS * IDX_DIM) ** -0.5
    beta = jax.nn.sigmoid(p[..., o + IDX_HEADS:o + IDX_HEADS + DELTA_HEADS])
    a = p[..., o + IDX_HEADS + DELTA_HEADS:o + IDX_HEADS + 2 * DELTA_HEADS]
    g = -jnp.exp(a_log.astype(F32)) * jax.nn.softplus(a + dt_bias.astype(F32))
    return qi, ki, wi, g, beta


def _split_projection(p, pos, conv_prev, conv_w, a_log, dt_bias):
    b, t, _ = p.shape
    q = _rope(p[..., COL_Q:COL_K].reshape(b, t, ATTN_HEADS, HEAD_DIM), pos)
    k = _rope(p[..., COL_K:COL_V].reshape(b, t, ATTN_HEADS, HEAD_DIM), pos)
    v = p[..., COL_V:COL_QI].reshape(b, t, ATTN_HEADS, HEAD_DIM)
    qi, ki, wi, g, beta = _small_projection(p, pos, a_log, dt_bias)
    dqkv = p[..., COL_DQKV:COL_Z]
    z = p[..., COL_Z:COL_END]
    xpad = jnp.concatenate([conv_prev.astype(F32), dqkv], axis=1)
    conv = sum(conv_w[i] * xpad[:, i:i + t] for i in range(CONV_WIDTH))
    conv = jax.nn.silu(conv)
    new_conv = xpad[:, t:]
    dq = _l2norm(conv[..., :DELTA_WIDTH].reshape(b, t, DELTA_HEADS, DELTA_DK)) * DELTA_DK ** -0.5
    dk = _l2norm(conv[..., DELTA_WIDTH:2 * DELTA_WIDTH].reshape(b, t, DELTA_HEADS, DELTA_DK))
    dv = conv[..., 2 * DELTA_WIDTH:]
    return (q, k, v, qi, ki, wi), (dq.reshape(b, t, DELTA_WIDTH), dk.reshape(b, t, DELTA_WIDTH), dv, g, beta, z), new_conv


def _pad_rows(x, tp):
    return jnp.pad(x, [(0, 0), (0, tp - x.shape[1])] + [(0, 0)] * (x.ndim - 2))


def _delta_rows(t):
    n_chunks = -(-t // DELTA_CHUNK)
    return n_chunks, n_chunks * DELTA_CHUNK


def _delta_group(qkvz, g, beta, s0, delta_norm):
    b, t, _ = g.shape
    n_chunks, tp = _delta_rows(t)

    def rows(x):
        return _pad_rows(x, tp).reshape(b, n_chunks, DELTA_CHUNK, DELTA_HEADS).transpose(0, 3, 1, 2)

    gcum = jnp.cumsum(rows(g), axis=-1)
    o, s = gated_delta(qkvz, gcum, rows(beta), s0.astype(F32), delta_norm, n_chunks=n_chunks)
    return o[:, :t], s


def _sample_attention(q, k, v, qi, ki, wi, cache_k, cache_v, cache_kidx, page_table, topk):
    db, ds, h, hd = q.shape
    n_pool = cache_k.shape[0]

    def new_page(x):
        return _pad_rows(x.reshape(db, ds, -1), PAGE_SIZE)

    scores, scores_new = sample_scores(
        qi.reshape(db, ds * IDX_HEADS, IDX_DIM).astype(MXU_DTYPE), wi.reshape(db, ds * IDX_HEADS, 1),
        new_page(ki), cache_kidx, page_table, ds=ds)
    pack = LANES // (ds * h)
    q_t = q.reshape(db, ds * h, hd).transpose(0, 2, 1)
    q_cols = jnp.concatenate(
        [jnp.pad(q_t, ((0, 0), (0, 0), (s * ds * h, LANES - (s + 1) * ds * h))) for s in range(pack)], axis=1)

    def key_head_rows(x):
        return x.reshape(x.shape[:-3] + (x.shape[-3] * h, hd))

    o = sample_attend(scores, scores_new, q_cols.astype(MXU_DTYPE),
                      key_head_rows(_pad_rows(k, PAGE_SIZE)), key_head_rows(_pad_rows(v, PAGE_SIZE)),
                      key_head_rows(cache_k), key_head_rows(cache_v), page_table, ds=ds, topk=topk)
    return o[:, :, :ds * h].transpose(0, 2, 1).reshape(db, ds, h * hd)


def _token_tile(n, want):
    if n % LANES:
        return n
    best = LANES
    for m in range(LANES, want + 1, LANES):
        if n % m == 0:
            best = m
    return best


def _layer(l, x, x_groups, dims, cache_k, cache_v, cache_kidx, state_conv, state_ssm, page_table,
           norm_mix, w_in, conv_w, a_log, dt_bias, delta_norm, w_out, norm_ffn,
           peer_wq, peer_subkeys, peer_u, peer_v, norm_out):
    b, t, db, ds = dims
    n_pad, d = x.shape
    n_p, n_s = b * t, db * ds
    past = page_table.shape[1] * PAGE_SIZE
    tm = _token_tile(n_pad, 1280)

    w = w_in[l]
    o_wi = 3 * ATTN_WIDTH + IDX_HEADS * IDX_DIM + IDX_DIM + IDX_HEADS
    o_z = o_wi + 3 * DELTA_WIDTH
    o_b = o_z + DELTA_WIDTH
    w_perm = jnp.concatenate(
        [w[:, :o_wi], w[:, o_b:o_b + 2 * DELTA_HEADS],
         jnp.zeros((d, COL_DQKV - (o_wi + 2 * DELTA_HEADS)), w.dtype), w[:, o_wi:o_b]], axis=1).astype(MXU_DTYPE)
    p_p = norm_matmul(x_groups[0], norm_mix[l], w_perm, tm=_token_tile(n_p, 1280), tn=1024, out_dtype=F32,
                      return_normed=False).reshape(b, t, PROJ_PAD)
    p_s = norm_matmul(x_groups[1], norm_mix[l], w_perm, tm=_token_tile(n_s, 1280), tn=1024, out_dtype=F32,
                      return_normed=False).reshape(db, ds, PROJ_PAD)

    pos_p = jnp.arange(t)
    tp = _round_up(t, ATTN_QBLOCK)
    topk_p = min(TOPK_MAX, (t - N_META) // 4)
    qkv, k_p, v_p = rope_cast(p_p, pos_p, tp=tp)
    k_p = k_p.reshape(b, t, ATTN_HEADS, HEAD_DIM)
    v_p = v_p.reshape(b, t, ATTN_HEADS, HEAD_DIM)
    qi, ki_p, wi, g, beta = _small_projection(p_p, pos_p, a_log[l], dt_bias[l])
    attn_p = attn_prompt(qkv, _pad_rows(qi, tp).transpose(0, 2, 1, 3).astype(MXU_DTYPE),
                         _pad_rows(ki_p, tp).astype(MXU_DTYPE), _pad_rows(wi, tp),
                         tq=ATTN_QBLOCK, topk=topk_p)[:, :t]
    conv_prev_p = jnp.zeros((b, CONV_WIDTH - 1, 3 * DELTA_WIDTH), F32)
    qkvz = delta_prep(p_p, conv_prev_p, conv_w[l], tp=_delta_rows(t)[1])
    conv_p = jnp.concatenate([conv_prev_p, p_p[:, -(CONV_WIDTH - 1):, COL_DQKV:COL_Z]], axis=1)[:, -(CONV_WIDTH - 1):]
    gated_p, ssm_p = _delta_group(qkvz, g, beta, jnp.zeros((b, DELTA_HEADS, DELTA_DK, DELTA_DV), F32), delta_norm[l])

    pos_s = past + jnp.arange(ds)
    (q, k_s, v_s, qi, ki_s, wi), (dq, dk, dv, g, beta, z), conv_s = _split_projection(
        p_s, pos_s, state_conv[l], conv_w[l], a_log[l], dt_bias[l])
    topk_s = min(TOPK_MAX, (past + ds) // 4)
    attn_s = _sample_attention(q, k_s, v_s, qi, ki_s, wi, cache_k[l], cache_v[l], cache_kidx[l],
                               page_table, topk_s).astype(MXU_DTYPE)
    tp_s = _delta_rows(ds)[1]
    gated_s, ssm_s = _delta_group(jnp.stack([_pad_rows(a_, tp_s) for a_ in (dq, dk, dv, z)]), g, beta,
                                  state_ssm[l], delta_norm[l])

    def flat(xp_, xs_):
        return jnp.concatenate([xp_.reshape(n_p, -1), xs_.reshape(n_s, -1),
                                jnp.zeros((n_pad - n_p - n_s, xp_.shape[-1]), xp_.dtype)], axis=0)

    wo = w_out[l].astype(MXU_DTYPE)
    x1 = matmul2_residual(flat(attn_p, attn_s), flat(gated_p, gated_s), wo[:ATTN_WIDTH], wo[ATTN_WIDTH:], x,
                          tm=tm, tn=1024 if d % 1024 == 0 else d)

    wq = peer_wq[l].astype(MXU_DTYPE)
    qp, xn = norm_matmul(x1, norm_ffn[l], wq, tm=tm, tn=wq.shape[1], out_dtype=MXU_DTYPE, return_normed=True)
    qp = qp.reshape(n_pad, 2 * PEER_HEADS, PEER_QDIM // 2).transpose(1, 0, 2)
    a, bb, misc = peer_select(qp, peer_subkeys[l].astype(MXU_DTYPE), tn=TOKEN_BLOCK)
    y = peer_dense(xn, a, bb, misc, peer_u[l].astype(MXU_DTYPE), peer_v[l].T.astype(MXU_DTYPE), x1, norm_out,
                   tn=TOKEN_BLOCK, te=PEER_EXPERT_BLOCK)
    caches_p = (k_p, v_p, ki_p, conv_p, ssm_p)
    caches_s = (k_s, v_s, ki_s, conv_s, ssm_s)
    return y, caches_p, caches_s


def kernel(x_prompt, x_sample, cache_k, cache_v, cache_kidx, state_conv, state_ssm, page_table, meta, norm_mix,
           w_in, conv_w, a_log, dt_bias, delta_norm, w_out, norm_ffn, peer_wq, peer_subkeys, peer_u, peer_v,
           norm_final):
    b, s, d = x_prompt.shape
    db, ds, _ = x_sample.shape
    t = s + N_META
    depth = w_in.shape[0]
    assert depth == 1, "the fused PEER + final-norm epilogue assumes a single layer"
    n_p, n_s = b * t, db * ds
    n_pad = _round_up(n_p + n_s, TOKEN_BLOCK)
    xp = jnp.concatenate([jnp.broadcast_to(meta[None].astype(F32), (b, N_META, d)), x_prompt], axis=1)
    x = jnp.concatenate([xp.reshape(n_p, d), x_sample.reshape(n_s, d), jnp.zeros((n_pad - n_p - n_s, d), F32)], axis=0)
    y, cp, cs = _layer(0, x, (xp.reshape(n_p, d), x_sample.reshape(n_s, d)), (b, t, db, ds), cache_k, cache_v, cache_kidx, state_conv, state_ssm, page_table,
                       norm_mix, w_in, conv_w, a_log, dt_bias, delta_norm, w_out, norm_ffn,
                       peer_wq, peer_subkeys, peer_u, peer_v, norm_final)
    y_prompt = y[:n_p].reshape(b, t, d)[:, N_META:]
    y_sample = y[n_p:n_p + n_s].reshape(db, ds, d)
    return (y_prompt, y_sample) + tuple(c[None] for c in cp) + tuple(c[None] for c in cs)
```
